```python
import jax, jax.numpy as jnp
from jax import lax
import numpy as np

D_MODEL = 4096
BATCH = 8
SEQ = 2048
DEPTH = 4
DEC_BATCH = 2
DEC_SEQ = 4096
PAST_LEN = 128

GRID_W = 64
BRANCH_W = D_MODEL // 2
N_BRANCH = 3
SGU_CHUNK = 128
SGU_GROUPS = 8
GLA_HEADS = 4
GLA_KEY_W = BRANCH_W // 2
GLA_HEAD_K = GLA_KEY_W // GLA_HEADS
GLA_HEAD_V = BRANCH_W // GLA_HEADS
GLA_LOW_RANK = 16
GLA_GATE_TEMP = 16.0
GLA_CHUNK = 64
NA_HEAD_DIM = 128
NA_HEADS = BRANCH_W // NA_HEAD_DIM
NA_ROWS_MAX = 8
NA_COLS = 16
EPS = 1e-6

IN_SPLIT_SIZES = (
    BRANCH_W, BRANCH_W, BRANCH_W,
    GLA_KEY_W, GLA_KEY_W, BRANCH_W, BRANCH_W,
    GLA_LOW_RANK, GLA_LOW_RANK,
    BRANCH_W, BRANCH_W, BRANCH_W, BRANCH_W,
)
IN_COLS = sum(IN_SPLIT_SIZES)

kernel_name = "hybrid_sgu_gla_natten_encoder"


def rmsnorm(x, g):
    xf = x.astype(jnp.float32)
    y = xf * lax.rsqrt(jnp.mean(xf * xf, axis=-1, keepdims=True) + EPS)
    return (y * g.astype(jnp.float32)).astype(x.dtype)


def layernorm(x, g):
    xf = x.astype(jnp.float32)
    mu = jnp.mean(xf, axis=-1, keepdims=True)
    xc = xf - mu
    y = xc * lax.rsqrt(jnp.mean(xc * xc, axis=-1, keepdims=True) + EPS)
    return (y * g.astype(jnp.float32)).astype(x.dtype)


def sgu_mixer(u, v, ln_g, w_s, b_s):
    B, T, _ = u.shape
    nc = T // SGU_CHUNK
    vn = layernorm(v, ln_g).reshape(B, nc, SGU_CHUNK, SGU_GROUPS, BRANCH_W // SGU_GROUPS)
    s = jnp.einsum('gij,bcjgd->bcigd', w_s, vn) + b_s.T[None, None, :, :, None]
    return u * s.reshape(B, T, BRANCH_W)


def _to_chunks(t, c):
    B, T, H, d = t.shape
    return t.reshape(B, T // c, c, H, d).transpose(1, 0, 3, 2, 4)


def gla_causal(q, k, v, g):
    B, T, H, dk = q.shape
    dv = v.shape[-1]
    f32 = jnp.float32
    qc, kc, vc, gc = [_to_chunks(t.astype(f32), GLA_CHUNK) for t in (q, k, v, g)]
    Gc = jnp.cumsum(gc, axis=-2)
    mask = jnp.tril(jnp.ones((GLA_CHUNK, GLA_CHUNK), dtype=bool))

    def step(S, inp):
        qi, ki, vi, Gi = inp
        Gl = Gi[..., -1:, :]
        q_e = qi * jnp.exp(Gi)
        k_e = ki * jnp.exp(-Gi)
        a = jnp.where(mask, jnp.einsum('bhid,bhjd->bhij', q_e, k_e), 0.0)
        o = jnp.einsum('bhij,bhjv->bhiv', a, vi) + jnp.einsum('bhid,bhdv->bhiv', q_e, S)
        S = jnp.exp(Gl[..., 0, :])[..., None] * S + jnp.einsum(
            'bhjd,bhjv->bhdv', ki * jnp.exp(Gl - Gi), vi)
        return S, o

    S0 = jnp.zeros((B, H, dk, dv), f32)
    _, o = lax.scan(step, S0, (qc, kc, vc, Gc))
    return o.transpose(1, 0, 3, 2, 4).reshape(B, T, H, dv).astype(v.dtype)


def gla_mixer(xq, xk, xv, lr_f, lr_b, w_gk, b_gk, norm_g):
    B, T, _ = xq.shape
    q = xq.reshape(B, T, GLA_HEADS, GLA_HEAD_K) * (GLA_HEAD_K ** -0.5)
    k = xk.reshape(B, T, GLA_HEADS, GLA_HEAD_K)
    v = xv.reshape(B, T, GLA_HEADS, GLA_HEAD_V)
    g_f = (jax.nn.log_sigmoid((lr_f @ w_gk[0] + b_gk[0]).astype(jnp.float32)) / GLA_GATE_TEMP
           ).reshape(B, T, GLA_HEADS, GLA_HEAD_K)
    g_b = (jax.nn.log_sigmoid((lr_b @ w_gk[1] + b_gk[1]).astype(jnp.float32)) / GLA_GATE_TEMP
           ).reshape(B, T, GLA_HEADS, GLA_HEAD_K)
    o_f = gla_causal(q, k, v, g_f)
    o_b = jnp.flip(gla_causal(jnp.flip(q, 1), jnp.flip(k, 1), jnp.flip(v, 1), jnp.flip(g_b, 1)), 1)
    o = rmsnorm(o_f + o_b, norm_g)
    return o.reshape(B, T, BRANCH_W)


def na_mixer(xq, xk, xv, rpb):
    B, T, _ = xq.shape
    rows = T // GRID_W
    kr = min(NA_ROWS_MAX, rows)
    q = xq.reshape(B, T, NA_HEADS, NA_HEAD_DIM) * (NA_HEAD_DIM ** -0.5)
    k = xk.reshape(B, T, NA_HEADS, NA_HEAD_DIM)
    v = xv.reshape(B, T, NA_HEADS, NA_HEAD_DIM)
    cols = jnp.arange(GRID_W)
    cs = jnp.clip(cols - NA_COLS // 2, 0, GRID_W - NA_COLS)
    col_ok = (cols[None, :] >= cs[:, None]) & (cols[None, :] < cs[:, None] + NA_COLS)
    dc_idx = jnp.clip(cols[None, :] - cols[:, None] + NA_COLS - 1, 0, 2 * NA_COLS - 2)
    neg = jnp.finfo(jnp.float32).min

    def row_block(r):
        rs = jnp.clip(r - kr // 2, 0, rows - kr)
        qr = lax.dynamic_slice_in_dim(q, r * GRID_W, GRID_W, axis=1)
        kb = lax.dynamic_slice_in_dim(k, rs * GRID_W, kr * GRID_W, axis=1
                                      ).reshape(B, kr, GRID_W, NA_HEADS, NA_HEAD_DIM)
        vb = lax.dynamic_slice_in_dim(v, rs * GRID_W, kr * GRID_W, axis=1
                                      ).reshape(B, kr, GRID_W, NA_HEADS, NA_HEAD_DIM)
        dr_idx = rs + jnp.arange(kr) - r + NA_ROWS_MAX - 1
        bias = rpb[:, dr_idx[:, None, None], dc_idx[None, :, :]]
        s = jnp.einsum('bqhd,bnkhd->bhqnk', qr, kb).astype(jnp.float32)
        s = s + bias.transpose(0, 2, 1, 3)[None].astype(jnp.float32)
        s = jnp.where(col_ok[None, None, :, None, :], s, neg)
        p = jax.nn.softmax(s.reshape(B, NA_HEADS, GRID_W, kr * GRID_W), axis=-1)
        p = p.reshape(B, NA_HEADS, GRID_W, kr, GRID_W).astype(v.dtype)
        return jnp.einsum('bhqnk,bnkhd->bqhd', p, vb)

    o = lax.map(row_block, jnp.arange(rows))
    return o.transpose(1, 0, 2, 3, 4).reshape(B, T, BRANCH_W)


def layer(x, norm_g, w_in, sgu_ln_g, sgu_w, sgu_b, gla_w_gk, gla_b_gk, gla_norm_g,
          na_rpb, w_gate, b_gate, w_branch, w_out):
    B, T, _ = x.shape
    h = rmsnorm(x, norm_g)
    split_idx = tuple(int(i) for i in np.cumsum(IN_SPLIT_SIZES)[:-1])
    (a_u, a_v, a_z, b_q, b_k, b_v, b_z, b_lf, b_lb,
     c_q, c_k, c_v, c_z) = jnp.split(h @ w_in, split_idx, axis=-1)
    y_a = sgu_mixer(a_u, a_v, sgu_ln_g, sgu_w, sgu_b) * jax.nn.silu(a_z)
    y_b = gla_mixer(b_q, b_k, b_v, b_lf, b_lb, gla_w_gk, gla_b_gk, gla_norm_g) * jax.nn.silu(b_z)
    y_c = na_mixer(c_q, c_k, c_v, na_rpb) * jax.nn.silu(c_z)
    ys = jnp.stack([y_a, y_b, y_c], axis=2)
    branch = jnp.einsum('btnw,nwd->btnd', ys, w_branch)
    gates = jax.nn.sigmoid(h @ w_gate + b_gate).reshape(B, T, N_BRANCH, D_MODEL)
    merged = jnp.sum(gates * branch, axis=2)
    return x + merged @ w_out


def trunk(x, norm_g, w_in, sgu_ln_g, sgu_w, sgu_b, gla_w_gk, gla_b_gk, gla_norm_g,
          na_rpb, w_gate, b_gate, w_branch, w_out, final_norm_g):
    for l in range(DEPTH):
        x = layer(x, norm_g[l], w_in[l], sgu_ln_g[l], sgu_w[l], sgu_b[l], gla_w_gk[l],
                  gla_b_gk[l], gla_norm_g[l], na_rpb[l], w_gate[l], b_gate[l],
                  w_branch[l], w_out[l])
    return rmsnorm(x, final_norm_g)


def setup_inputs(seed: int = 0) -> dict:
    key = jax.random.key(seed)
    ks = jax.random.split(key, 16)
    n = jax.random.normal
    f32 = jnp.float32
    return {
        "x_prompt": n(ks[0], (BATCH, SEQ, D_MODEL), f32),
        "x_sample": n(ks[1], (DEC_BATCH, DEC_SEQ, D_MODEL), f32),
        "norm_g": 1.0 + 0.05 * n(ks[2], (DEPTH, D_MODEL), f32),
        "w_in": n(ks[3], (DEPTH, D_MODEL, IN_COLS), f32) * (D_MODEL ** -0.5),
        "sgu_ln_g": 1.0 + 0.05 * n(ks[4], (DEPTH, BRANCH_W), f32),
        "sgu_w": n(ks[5], (DEPTH, SGU_GROUPS, SGU_CHUNK, SGU_CHUNK), f32) * (SGU_CHUNK ** -0.5),
        "sgu_b": 1.0 + 0.1 * n(ks[6], (DEPTH, SGU_GROUPS, SGU_CHUNK), f32),
        "gla_w_gk": n(ks[7], (DEPTH, 2, GLA_LOW_RANK, GLA_KEY_W), f32) * (GLA_LOW_RANK ** -0.5),
        "gla_b_gk": 0.1 * n(ks[8], (DEPTH, 2, GLA_KEY_W), f32),
        "gla_norm_g": 1.0 + 0.05 * n(ks[9], (DEPTH, GLA_HEAD_V), f32),
        "na_rpb": 0.1 * n(ks[10], (DEPTH, NA_HEADS, 2 * NA_ROWS_MAX - 1, 2 * NA_COLS - 1), f32),
        "w_gate": n(ks[11], (DEPTH, D_MODEL, N_BRANCH * D_MODEL), f32) * (D_MODEL ** -0.5),
        "b_gate": 0.1 * n(ks[12], (DEPTH, N_BRANCH * D_MODEL), f32),
        "w_branch": n(ks[13], (DEPTH, N_BRANCH, BRANCH_W, D_MODEL), f32) * (BRANCH_W ** -0.5),
        "w_out": n(ks[14], (DEPTH, D_MODEL, D_MODEL), f32) * (D_MODEL ** -0.5),
        "final_norm_g": 1.0 + 0.05 * n(ks[15], (D_MODEL,), f32),
    }


def reference(x_prompt, x_sample, norm_g, w_in, sgu_ln_g, sgu_w, sgu_b, gla_w_gk, gla_b_gk,
              gla_norm_g, na_rpb, w_gate, b_gate, w_branch, w_out, final_norm_g):
    y_prompt = trunk(x_prompt, norm_g, w_in, sgu_ln_g, sgu_w, sgu_b, gla_w_gk, gla_b_gk,
                     gla_norm_g, na_rpb, w_gate, b_gate, w_branch, w_out, final_norm_g)
    y_sample = trunk(x_sample, norm_g, w_in, sgu_ln_g, sgu_w, sgu_b, gla_w_gk, gla_b_gk,
                     gla_norm_g, na_rpb, w_gate, b_gate, w_branch, w_out, final_norm_g)
    return (y_prompt, y_sample)
```

```python
import functools

import jax
import jax.numpy as jnp
from jax import lax
from jax.experimental import pallas as pl
from jax.experimental.pallas import tpu as pltpu

F32 = jnp.float32
BF16 = jnp.bfloat16

D_MODEL = 4096
DEPTH = 4
BRANCH_W = D_MODEL // 2
N_BRANCH = 3
GRID_W = 64
SGU_CHUNK = 128
SGU_GROUPS = 8
SGU_GROUP_W = BRANCH_W // SGU_GROUPS
GLA_HEADS = 4
GLA_KEY_W = BRANCH_W // 2
GLA_HEAD_K = GLA_KEY_W // GLA_HEADS
GLA_HEAD_V = BRANCH_W // GLA_HEADS
GLA_LOW_RANK = 16
GLA_GATE_TEMP = 16.0
GLA_CHUNK = 64
NA_HEAD_DIM = 128
NA_HEADS = BRANCH_W // NA_HEAD_DIM
NA_ROWS = 8
NA_COLS = 16
EPS = 1e-6

COL_A_U = 0
COL_A_V = COL_A_U + BRANCH_W
COL_A_Z = COL_A_V + BRANCH_W
COL_B_Q = COL_A_Z + BRANCH_W
COL_B_K = COL_B_Q + GLA_KEY_W
COL_B_V = COL_B_K + GLA_KEY_W
COL_B_Z = COL_B_V + BRANCH_W
COL_C_Q = COL_B_Z + BRANCH_W
COL_C_K = COL_C_Q + BRANCH_W
COL_C_V = COL_C_K + BRANCH_W
COL_C_Z = COL_C_V + BRANCH_W
COL_GATE = COL_C_Z + BRANCH_W
PROJ_COLS = COL_GATE + N_BRANCH * D_MODEL
W_IN_LR_START = COL_B_Z + BRANCH_W
LR_PAD = 128

V7X_LANES = 128
VMEM_LIMIT = 56 * 1024 * 1024
NEG_BIAS = -1e30

PROJ_TM, PROJ_TN = 1024, 1024
NORM_TM = 256
SGU_TM = 256
GLA_TB = 256
MERGE_TM, MERGE_TN = 1024, 256
OUT_TM, OUT_TN = 1024, 512


def _params(sem):
    return pltpu.CompilerParams(dimension_semantics=sem, vmem_limit_bytes=VMEM_LIMIT)


def _silu(z):
    return z * jax.nn.sigmoid(z)


def _rmsnorm_kernel(x_ref, g_ref, o_ref):
    x = x_ref[...]
    ms = jnp.mean(x * x, axis=-1, keepdims=True)
    o_ref[...] = (x * lax.rsqrt(ms + EPS) * g_ref[...]).astype(o_ref.dtype)


def _rmsnorm(x, g, out_dtype):
    m = x.shape[0]
    return pl.pallas_call(
        _rmsnorm_kernel,
        out_shape=jax.ShapeDtypeStruct((m, D_MODEL), out_dtype),
        grid=(m // NORM_TM,),
        in_specs=[pl.BlockSpec((NORM_TM, D_MODEL), lambda i: (i, 0)),
                  pl.BlockSpec((1, D_MODEL), lambda i: (0, 0))],
        out_specs=pl.BlockSpec((NORM_TM, D_MODEL), lambda i: (i, 0)),
        compiler_params=_params(("parallel",)),
        name="rmsnorm",
    )(x, g.reshape(1, D_MODEL))


def _proj_kernel(h_ref, w_ref, wlr_ref, b_ref, o_ref, olr_ref, *, gate_tile0):
    j = pl.program_id(1)
    h = h_ref[...]
    acc = jnp.dot(h, w_ref[...], preferred_element_type=F32)

    @pl.when(j < gate_tile0)
    def _():
        o_ref[...] = acc.astype(o_ref.dtype)

    @pl.when(j >= gate_tile0)
    def _():
        o_ref[...] = jax.nn.sigmoid(acc + b_ref[...]).astype(o_ref.dtype)

    @pl.when(j == 0)
    def _():
        olr_ref[...] = jnp.dot(h, wlr_ref[...], preferred_element_type=F32).astype(olr_ref.dtype)


def _proj(h, w_main, w_lr, b_gate):
    m = h.shape[0]
    gate_tile0 = COL_GATE // PROJ_TN
    return pl.pallas_call(
        functools.partial(_proj_kernel, gate_tile0=gate_tile0),
        out_shape=(jax.ShapeDtypeStruct((m, PROJ_COLS), BF16),
                   jax.ShapeDtypeStruct((m, LR_PAD), BF16)),
        grid=(m // PROJ_TM, PROJ_COLS // PROJ_TN),
        in_specs=[pl.BlockSpec((PROJ_TM, D_MODEL), lambda i, j: (i, 0)),
                  pl.BlockSpec((D_MODEL, PROJ_TN), lambda i, j: (0, j)),
                  pl.BlockSpec((D_MODEL, LR_PAD), lambda i, j: (0, 0)),
                  pl.BlockSpec((1, PROJ_TN), lambda i, j: (0, jnp.maximum(j - gate_tile0, 0)))],
        out_specs=(pl.BlockSpec((PROJ_TM, PROJ_TN), lambda i, j: (i, j)),
                   pl.BlockSpec((PROJ_TM, LR_PAD), lambda i, j: (i, 0))),
        compiler_params=_params(("parallel", "arbitrary")),
        name="proj",
    )(h, w_main, w_lr, b_gate)


def _sgu_kernel(u_ref, v_ref, z_ref, lng_ref, ws_ref, bs_ref, o_ref, vn_ref):
    v = v_ref[...].astype(F32)
    mu = jnp.mean(v, axis=-1, keepdims=True)
    xc = v - mu
    var = jnp.mean(xc * xc, axis=-1, keepdims=True)
    vn_ref[...] = (xc * lax.rsqrt(var + EPS) * lng_ref[...]).astype(vn_ref.dtype)
    for c in range(SGU_TM // SGU_CHUNK):
        rows = slice(c * SGU_CHUNK, (c + 1) * SGU_CHUNK)
        for g in range(SGU_GROUPS):
            cols = slice(g * SGU_GROUP_W, (g + 1) * SGU_GROUP_W)
            s = jnp.dot(ws_ref[g], vn_ref[rows, cols], preferred_element_type=F32) + bs_ref[:, cols]
            u = u_ref[rows, cols].astype(F32)
            z = z_ref[rows, cols].astype(F32)
            o_ref[rows, cols] = (u * s * _silu(z)).astype(o_ref.dtype)


def _sgu(proj, ln_g, ws, bs_full):
    m = proj.shape[0]
    nblk = BRANCH_W // BRANCH_W
    del nblk
    return pl.pallas_call(
        _sgu_kernel,
        out_shape=jax.ShapeDtypeStruct((m, BRANCH_W), BF16),
        grid=(m // SGU_TM,),
        in_specs=[pl.BlockSpec((SGU_TM, BRANCH_W), lambda i: (i, COL_A_U // BRANCH_W)),
                  pl.BlockSpec((SGU_TM, BRANCH_W), lambda i: (i, COL_A_V // BRANCH_W)),
                  pl.BlockSpec((SGU_TM, BRANCH_W), lambda i: (i, COL_A_Z // BRANCH_W)),
                  pl.BlockSpec((1, BRANCH_W), lambda i: (0, 0)),
                  pl.BlockSpec((SGU_GROUPS, SGU_CHUNK, SGU_CHUNK), lambda i: (0, 0, 0)),
                  pl.BlockSpec((SGU_CHUNK, BRANCH_W), lambda i: (0, 0))],
        out_specs=pl.BlockSpec((SGU_TM, BRANCH_W), lambda i: (i, 0)),
        scratch_shapes=[pltpu.VMEM((SGU_TM, BRANCH_W), BF16)],
        compiler_params=_params(("parallel",)),
        name="sgu",
    )(proj, proj, proj, ln_g, ws, bs_full)


_NT = (((1,), (1,)), ((), ()))
_TN = (((0,), (0,)), ((), ()))


def _split3_dot(mat, x):
    x1 = x.astype(BF16)
    r1 = x - x1.astype(F32)
    x2 = r1.astype(BF16)
    x3 = (r1 - x2.astype(F32)).astype(BF16)
    return (jnp.dot(mat, x1, preferred_element_type=F32)
            + jnp.dot(mat, x2, preferred_element_type=F32)
            + jnp.dot(mat, x3, preferred_element_type=F32))


def _gla_block(reverse, j, nb, q_ref, k_ref, v_ref, z_ref, lr_ref, wgk_ref, bgk_ref, ng_ref,
               o_ref, st_ref, of_ref):
    tb = GLA_TB
    nchunk = tb // GLA_CHUNK
    jj = (nb - 1 - j) if reverse else j
    x = jnp.dot(lr_ref[...], wgk_ref[0], preferred_element_type=F32) + bgk_ref[0]
    g = (jnp.minimum(x, 0.0) - jnp.log1p(jnp.exp(-jnp.abs(x)))) * (1.0 / GLA_GATE_TEMP)

    ri = lax.broadcasted_iota(jnp.int32, (tb, tb), 0)
    ci = lax.broadcasted_iota(jnp.int32, (tb, tb), 1)
    same = (ri // GLA_CHUNK) == (ci // GLA_CHUNK)
    tri = jnp.logical_and(same, (ci >= ri) if reverse else (ci <= ri))
    m_tri = jnp.where(tri, 1.0, 0.0).astype(BF16)
    m_all = jnp.where(same, 1.0, 0.0).astype(BF16)
    gc = _split3_dot(m_tri, g)
    gt = _split3_dot(m_all, g)

    q = q_ref[...].astype(F32) * (GLA_HEAD_K ** -0.5)
    k = k_ref[...].astype(F32)
    v = v_ref[...]
    q_e = (q * jnp.exp(gc)).astype(BF16)
    k_e = (k * jnp.exp(-gc)).astype(BF16)
    k_d = (k * jnp.exp(gt - gc)).astype(BF16)
    a = lax.dot_general(q_e, k_e, _NT, preferred_element_type=F32)
    a = jnp.where(tri, a, 0.0).astype(BF16)
    o_intra = jnp.dot(a, v, preferred_element_type=F32)

    pieces = [None] * nchunk
    order = range(nchunk - 1, -1, -1) if reverse else range(nchunk)
    for c in order:
        rows = slice(c * GLA_CHUNK, (c + 1) * GLA_CHUNK)
        st = st_ref[...]
        pieces[c] = o_intra[rows] + lax.dot_general(q_e[rows], st.astype(BF16), _NT,
                                                    preferred_element_type=F32)
        dec = jnp.exp(gt[c * GLA_CHUNK:c * GLA_CHUNK + 1, :])
        upd = lax.dot_general(v[rows], k_d[rows], _TN, preferred_element_type=F32)
        st_ref[...] = st * dec + upd
    o_blk = jnp.concatenate(pieces, axis=0)

    tok = pl.multiple_of(jj * tb, tb)
    if not reverse:
        of_ref[pl.ds(tok, tb), :] = o_blk
    else:
        tot = of_ref[pl.ds(tok, tb), :] + o_blk
        ms = jnp.mean(tot * tot, axis=-1, keepdims=True)
        y = tot * lax.rsqrt(ms + EPS) * ng_ref[...]
        o_ref[...] = (y * _silu(z_ref[...].astype(F32))).astype(o_ref.dtype)


def _gla_kernel(q_ref, k_ref, v_ref, z_ref, lr_ref, wgk_ref, bgk_ref, ng_ref, o_ref, st_ref, of_ref,
                *, nb):
    d = pl.program_id(2)
    j = pl.program_id(3)

    @pl.when(j == 0)
    def _():
        st_ref[...] = jnp.zeros_like(st_ref)

    args = (q_ref, k_ref, v_ref, z_ref, lr_ref, wgk_ref, bgk_ref, ng_ref, o_ref, st_ref, of_ref)

    @pl.when(d == 0)
    def _():
        _gla_block(False, j, nb, *args)

    @pl.when(d == 1)
    def _():
        _gla_block(True, j, nb, *args)


def _gla(proj, lr, wgk_pad, bgk, norm_g, batch, seq):
    m = proj.shape[0]
    nb = seq // GLA_TB

    def tok_blk(b, d, j):
        return b * nb + j + d * (nb - 1 - 2 * j)

    kq, kv = GLA_HEAD_K, GLA_HEAD_V
    return pl.pallas_call(
        functools.partial(_gla_kernel, nb=nb),
        out_shape=jax.ShapeDtypeStruct((m, BRANCH_W), BF16),
        grid=(batch, GLA_HEADS, 2, nb),
        in_specs=[pl.BlockSpec((GLA_TB, kq), lambda b, h, d, j: (tok_blk(b, d, j), COL_B_Q // kq + h)),
                  pl.BlockSpec((GLA_TB, kq), lambda b, h, d, j: (tok_blk(b, d, j), COL_B_K // kq + h)),
                  pl.BlockSpec((GLA_TB, kv), lambda b, h, d, j: (tok_blk(b, d, j), COL_B_V // kv + h)),
                  pl.BlockSpec((GLA_TB, kv), lambda b, h, d, j: (tok_blk(b, d, j), COL_B_Z // kv + h)),
                  pl.BlockSpec((GLA_TB, LR_PAD), lambda b, h, d, j: (tok_blk(b, d, j), 0)),
                  pl.BlockSpec((1, LR_PAD, kq), lambda b, h, d, j: (d, 0, h)),
                  pl.BlockSpec((1, 1, kq), lambda b, h, d, j: (d, 0, h)),
                  pl.BlockSpec((1, kv), lambda b, h, d, j: (0, 0))],
        out_specs=pl.BlockSpec((GLA_TB, kv), lambda b, h, d, j: (b * nb + nb - 1 - d * j, h)),
        scratch_shapes=[pltpu.VMEM((kv, kq), F32), pltpu.VMEM((seq, kv), F32)],
        compiler_params=_params(("parallel", "parallel", "arbitrary", "arbitrary")),
        name="gla",
    )(proj, proj, proj, proj, lr, wgk_pad, bgk, norm_g)


def _na_kernel(q_ref, k_ref, v_ref, z_ref, bias_ref, o_ref, *, rows):
    win = NA_ROWS * GRID_W

    def body(r, carry):
        rs = jnp.clip(r - NA_ROWS // 2, 0, rows - NA_ROWS)
        q0 = pl.multiple_of(r * GRID_W, GRID_W)
        k0 = pl.multiple_of(rs * GRID_W, GRID_W)
        q = q_ref[pl.ds(q0, GRID_W), :]
        kw = k_ref[pl.ds(k0, win), :]
        vw = v_ref[pl.ds(k0, win), :]
        s = lax.dot_general(q, kw, _NT, preferred_element_type=F32) * (NA_HEAD_DIM ** -0.5)
        s = s + bias_ref[0, r - rs]
        mx = jnp.max(s, axis=-1, keepdims=True)
        p = jnp.exp(s - mx)
        l = jnp.sum(p, axis=-1, keepdims=True)
        o = jnp.dot(p.astype(BF16), vw, preferred_element_type=F32) / l
        z = z_ref[pl.ds(q0, GRID_W), :].astype(F32)
        o_ref[pl.ds(q0, GRID_W), :] = (o * _silu(z)).astype(o_ref.dtype)
        return carry

    lax.fori_loop(0, rows, body, 0, unroll=2)


def _na(proj, bias_tab, batch, seq):
    m = proj.shape[0]
    rows = seq // GRID_W
    hd = NA_HEAD_DIM
    return pl.pallas_call(
        functools.partial(_na_kernel, rows=rows),
        out_shape=jax.ShapeDtypeStruct((m, BRANCH_W), BF16),
        grid=(NA_HEADS, batch),
        in_specs=[pl.BlockSpec((seq, hd), lambda h, b: (b, COL_C_Q // hd + h)),
                  pl.BlockSpec((seq, hd), lambda h, b: (b, COL_C_K // hd + h)),
                  pl.BlockSpec((seq, hd), lambda h, b: (b, COL_C_V // hd + h)),
                  pl.BlockSpec((seq, hd), lambda h, b: (b, COL_C_Z // hd + h)),
                  pl.BlockSpec((1, NA_ROWS, GRID_W, NA_ROWS * GRID_W), lambda h, b: (h, 0, 0, 0))],
        out_specs=pl.BlockSpec((seq, hd), lambda h, b: (b, h)),
        compiler_params=_params(("parallel", "parallel")),
        name="na",
    )(proj, proj, proj, proj, bias_tab)


def _merge_kernel(ya_ref, yb_ref, yc_ref, wb_ref, ga_ref, gb_ref, gc_ref, o_ref):
    acc = ga_ref[...].astype(F32) * jnp.dot(ya_ref[...], wb_ref[0], preferred_element_type=F32)
    acc += gb_ref[...].astype(F32) * jnp.dot(yb_ref[...], wb_ref[1], preferred_element_type=F32)
    acc += gc_ref[...].astype(F32) * jnp.dot(yc_ref[...], wb_ref[2], preferred_element_type=F32)
    o_ref[...] = acc.astype(o_ref.dtype)


def _merge(ya, yb, yc, wb, proj):
    m = ya.shape[0]
    tm, tn = MERGE_TM, MERGE_TN
    y_spec = pl.BlockSpec((tm, BRANCH_W), lambda i, j: (i, 0))

    def gate_spec(b):
        return pl.BlockSpec((tm, tn), lambda i, j: (i, (COL_GATE + b * D_MODEL) // tn + j))

    return pl.pallas_call(
        _merge_kernel,
        out_shape=jax.ShapeDtypeStruct((m, D_MODEL), BF16),
        grid=(m // tm, D_MODEL // tn),
        in_specs=[y_spec, y_spec, y_spec,
                  pl.BlockSpec((N_BRANCH, BRANCH_W, tn), lambda i, j: (0, 0, j)),
                  gate_spec(0), gate_spec(1), gate_spec(2)],
        out_specs=pl.BlockSpec((tm, tn), lambda i, j: (i, j)),
        compiler_params=_params(("parallel", "arbitrary")),
        name="merge",
    )(ya, yb, yc, wb, proj, proj, proj)


def _out_kernel(m_ref, w_ref, x_ref, o_ref):
    o_ref[...] = x_ref[...] + jnp.dot(m_ref[...], w_ref[...], preferred_element_type=F32)


def _out(merged, wo, x):
    m = x.shape[0]
    tm, tn = OUT_TM, OUT_TN
    return pl.pallas_call(
        _out_kernel,
        out_shape=jax.ShapeDtypeStruct((m, D_MODEL), F32),
        grid=(m // tm, D_MODEL // tn),
        in_specs=[pl.BlockSpec((tm, D_MODEL), lambda i, j: (i, 0)),
                  pl.BlockSpec((D_MODEL, tn), lambda i, j: (0, j)),
                  pl.BlockSpec((tm, tn), lambda i, j: (i, j))],
        out_specs=pl.BlockSpec((tm, tn), lambda i, j: (i, j)),
        compiler_params=_params(("parallel", "arbitrary")),
        name="outproj",
    )(merged, wo, x)


def _na_bias_table(rpb):
    cols = jnp.arange(GRID_W)
    cs = jnp.clip(cols - NA_COLS // 2, 0, GRID_W - NA_COLS)
    col_ok = (cols[None, :] >= cs[:, None]) & (cols[None, :] < cs[:, None] + NA_COLS)
    dc = jnp.clip(cols[None, :] - cols[:, None] + NA_COLS - 1, 0, 2 * NA_COLS - 2)
    delta = jnp.arange(NA_ROWS)
    n = jnp.arange(NA_ROWS)
    dr = n[None, :] - delta[:, None] + NA_ROWS - 1
    tab = rpb[:, :, dr[:, :, None, None], dc[None, None, :, :]]
    tab = jnp.where(col_ok[None, None, None, None], tab, NEG_BIAS)
    tab = tab.transpose(0, 1, 2, 4, 3, 5)
    return tab.reshape(rpb.shape[0], NA_HEADS, NA_ROWS, GRID_W, NA_ROWS * GRID_W).astype(F32)


def _trunk(x, batch, seq, p):
    h = _rmsnorm(x, p["norm_g"][0], BF16)
    for l in range(DEPTH):
        proj, lr = _proj(h, p["w_main"][l], p["w_lr"][l], p["b_gate"][l])
        ya = _sgu(proj, p["sgu_ln_g"][l], p["sgu_w"][l], p["sgu_b"][l])
        yb = _gla(proj, lr, p["wgk"][l], p["bgk"][l], p["gla_norm_g"][l], batch, seq)
        yc = _na(proj, p["na_bias"][l], batch, seq)
        merged = _merge(ya, yb, yc, p["w_branch"][l], proj)
        x = _out(merged, p["w_out"][l], x)
        if l + 1 < DEPTH:
            h = _rmsnorm(x, p["norm_g"][l + 1], BF16)
    return _rmsnorm(x, p["final_norm_g"], F32)


def kernel(x_prompt, x_sample, norm_g, w_in, sgu_ln_g, sgu_w, sgu_b, gla_w_gk, gla_b_gk, gla_norm_g,
           na_rpb, w_gate, b_gate, w_branch, w_out, final_norm_g):
    lr0 = W_IN_LR_START
    lr1 = lr0 + 2 * GLA_LOW_RANK
    w_main = jnp.concatenate([w_in[:, :, :lr0], w_in[:, :, lr1:], w_gate], axis=2).astype(BF16)
    w_lr = jnp.pad(w_in[:, :, lr0:lr1], ((0, 0), (0, 0), (0, LR_PAD - 2 * GLA_LOW_RANK))).astype(BF16)
    wgk = jnp.zeros((DEPTH, 2, LR_PAD, GLA_KEY_W), F32)
    wgk = wgk.at[:, 0, :GLA_LOW_RANK].set(gla_w_gk[:, 0])
    wgk = wgk.at[:, 1, GLA_LOW_RANK:2 * GLA_LOW_RANK].set(gla_w_gk[:, 1])
    p = {
        "norm_g": norm_g,
        "final_norm_g": final_norm_g,
        "w_main": w_main,
        "w_lr": w_lr,
        "b_gate": b_gate.reshape(DEPTH, 1, N_BRANCH * D_MODEL),
        "sgu_ln_g": sgu_ln_g.reshape(DEPTH, 1, BRANCH_W),
        "sgu_w": sgu_w.astype(BF16),
        "sgu_b": jnp.repeat(jnp.swapaxes(sgu_b, 1, 2), SGU_GROUP_W, axis=2),
        "wgk": wgk.astype(BF16),
        "bgk": gla_b_gk.reshape(DEPTH, 2, 1, GLA_KEY_W),
        "gla_norm_g": gla_norm_g.reshape(DEPTH, 1, GLA_HEAD_V),
        "na_bias": _na_bias_table(na_rpb),
        "w_branch": w_branch.astype(BF16),
        "w_out": w_out.astype(BF16),
    }
    bp, tp, _ = x_prompt.shape
    bs, ts, _ = x_sample.shape
    y_prompt = _trunk(x_prompt.reshape(bp * tp, D_MODEL), bp, tp, p).reshape(x_prompt.shape)
    y_sample = _trunk(x_sample.reshape(bs * ts, D_MODEL), bs, ts, p).reshape(x_sample.shape)
    return (y_prompt, y_sample)
```

```python
import functools

import jax
import jax.numpy as jnp
from jax import lax
from jax.experimental import pallas as pl
from jax.experimental.pallas import tpu as pltpu

F32 = jnp.float32
BF16 = jnp.bfloat16

D_MODEL = 4096
DEPTH = 4
BRANCH_W = D_MODEL // 2
N_BRANCH = 3
GRID_W = 64
SGU_CHUNK = 128
SGU_GROUPS = 8
SGU_GROUP_W = BRANCH_W // SGU_GROUPS
GLA_HEADS = 4
GLA_KEY_W = BRANCH_W // 2
GLA_HEAD_K = GLA_KEY_W // GLA_HEADS
GLA_HEAD_V = BRANCH_W // GLA_HEADS
GLA_LOW_RANK = 16
GLA_GATE_TEMP = 16.0
GLA_CHUNK = 64
NA_HEAD_DIM = 128
NA_HEADS = BRANCH_W // NA_HEAD_DIM
NA_ROWS = 8
NA_COLS = 16
EPS = 1e-6

COL_A_U = 0
COL_A_V = COL_A_U + BRANCH_W
COL_A_Z = COL_A_V + BRANCH_W
COL_B_Q = COL_A_Z + BRANCH_W
COL_B_K = COL_B_Q + GLA_KEY_W
COL_B_V = COL_B_K + GLA_KEY_W
COL_B_Z = COL_B_V + BRANCH_W
AB_COLS = COL_B_Z + BRANCH_W
COL_C_Q = 0
COL_C_K = COL_C_Q + BRANCH_W
COL_C_V = COL_C_K + BRANCH_W
COL_C_Z = COL_C_V + BRANCH_W
C_COLS = COL_C_Z + BRANCH_W
LR_PAD = 128

V7X_LANES = 128
VMEM_LIMIT = 56 * 1024 * 1024
NEG_BIAS = -1e30

PROJ_TM, PROJ_TN = 1024, 1024
PROJ_CHUNK = 256
NORM_TM = 256
SGU_TM = 256
GLA_TB = 256
NA_GROUP = 8
MERGE_TM, MERGE_TN = 1024, 512
OUT_TM, OUT_TN = 1024, 1024


def _params(sem):
    return pltpu.CompilerParams(dimension_semantics=sem, vmem_limit_bytes=VMEM_LIMIT)


def _silu(z):
    return z * jax.nn.sigmoid(z)


def _rmsnorm_kernel(x_ref, g_ref, o_ref):
    x = x_ref[...]
    ms = jnp.mean(x * x, axis=-1, keepdims=True)
    o_ref[...] = (x * lax.rsqrt(ms + EPS) * g_ref[...]).astype(o_ref.dtype)


def _rmsnorm(x, g, out_dtype):
    m = x.shape[0]
    return pl.pallas_call(
        _rmsnorm_kernel,
        out_shape=jax.ShapeDtypeStruct((m, D_MODEL), out_dtype),
        grid=(m // NORM_TM,),
        in_specs=[pl.BlockSpec((NORM_TM, D_MODEL), lambda i: (i, 0)),
                  pl.BlockSpec((1, D_MODEL), lambda i: (0, 0))],
        out_specs=pl.BlockSpec((NORM_TM, D_MODEL), lambda i: (i, 0)),
        compiler_params=_params(("parallel",)),
        name="rmsnorm",
    )(x, g.reshape(1, D_MODEL))


def _dot_chunks(h_ref, w_ref, o_ref, epilogue):
    for c in range(o_ref.shape[1] // PROJ_CHUNK):
        cols = slice(c * PROJ_CHUNK, (c + 1) * PROJ_CHUNK)
        acc = jnp.dot(h_ref[...], w_ref[:, cols], preferred_element_type=F32)
        o_ref[:, cols] = epilogue(acc, cols).astype(o_ref.dtype)


def _proj_kernel(h_ref, w_ref, o_ref):
    _dot_chunks(h_ref, w_ref, o_ref, lambda acc, cols: acc)


def _proj_lr_kernel(h_ref, w_ref, wlr_ref, o_ref, olr_ref):
    _dot_chunks(h_ref, w_ref, o_ref, lambda acc, cols: acc)

    @pl.when(pl.program_id(1) == 0)
    def _():
        olr_ref[...] = jnp.dot(h_ref[...], wlr_ref[...], preferred_element_type=F32).astype(olr_ref.dtype)


def _proj_gate_kernel(h_ref, w_ref, b_ref, o_ref):
    _dot_chunks(h_ref, w_ref, o_ref, lambda acc, cols: jax.nn.sigmoid(acc + b_ref[:, cols]))


def _proj(h, w, w_lr=None, b_gate=None):
    m = h.shape[0]
    n = w.shape[1]
    h_spec = pl.BlockSpec((PROJ_TM, D_MODEL), lambda i, j: (i, 0))
    w_spec = pl.BlockSpec((D_MODEL, PROJ_TN), lambda i, j: (0, j))
    o_spec = pl.BlockSpec((PROJ_TM, PROJ_TN), lambda i, j: (i, j))
    o_shape = jax.ShapeDtypeStruct((m, n), BF16)
    common = dict(grid=(m // PROJ_TM, n // PROJ_TN), compiler_params=_params(("parallel", "arbitrary")))
    if w_lr is not None:
        return pl.pallas_call(
            _proj_lr_kernel,
            out_shape=(o_shape, jax.ShapeDtypeStruct((m, LR_PAD), BF16)),
            in_specs=[h_spec, w_spec, pl.BlockSpec((D_MODEL, LR_PAD), lambda i, j: (0, 0))],
            out_specs=(o_spec, pl.BlockSpec((PROJ_TM, LR_PAD), lambda i, j: (i, 0))),
            name="proj_ab", **common)(h, w, w_lr)
    if b_gate is not None:
        return pl.pallas_call(
            _proj_gate_kernel, out_shape=o_shape,
            in_specs=[h_spec, w_spec, pl.BlockSpec((1, PROJ_TN), lambda i, j: (0, j))],
            out_specs=o_spec, name="proj_gate", **common)(h, w, b_gate)
    return pl.pallas_call(
        _proj_kernel, out_shape=o_shape, in_specs=[h_spec, w_spec], out_specs=o_spec,
        name="proj_c", **common)(h, w)


def _sgu_kernel(u_ref, v_ref, z_ref, lng_ref, ws_ref, bs_ref, o_ref, vn_ref):
    v = v_ref[...].astype(F32)
    mu = jnp.mean(v, axis=-1, keepdims=True)
    xc = v - mu
    var = jnp.mean(xc * xc, axis=-1, keepdims=True)
    vn_ref[...] = (xc * lax.rsqrt(var + EPS) * lng_ref[...]).astype(vn_ref.dtype)
    for c in range(SGU_TM // SGU_CHUNK):
        rows = slice(c * SGU_CHUNK, (c + 1) * SGU_CHUNK)
        for g in range(SGU_GROUPS):
            cols = slice(g * SGU_GROUP_W, (g + 1) * SGU_GROUP_W)
            s = jnp.dot(ws_ref[g], vn_ref[rows, cols], preferred_element_type=F32) + bs_ref[:, cols]
            u = u_ref[rows, cols].astype(F32)
            z = z_ref[rows, cols].astype(F32)
            o_ref[rows, cols] = (u * s * _silu(z)).astype(o_ref.dtype)


def _sgu(proj, ln_g, ws, bs_full):
    m = proj.shape[0]
    return pl.pallas_call(
        _sgu_kernel,
        out_shape=jax.ShapeDtypeStruct((m, BRANCH_W), BF16),
        grid=(m // SGU_TM,),
        in_specs=[pl.BlockSpec((SGU_TM, BRANCH_W), lambda i: (i, COL_A_U // BRANCH_W)),
                  pl.BlockSpec((SGU_TM, BRANCH_W), lambda i: (i, COL_A_V // BRANCH_W)),
                  pl.BlockSpec((SGU_TM, BRANCH_W), lambda i: (i, COL_A_Z // BRANCH_W)),
                  pl.BlockSpec((1, BRANCH_W), lambda i: (0, 0)),
                  pl.BlockSpec((SGU_GROUPS, SGU_CHUNK, SGU_CHUNK), lambda i: (0, 0, 0)),
                  pl.BlockSpec((SGU_CHUNK, BRANCH_W), lambda i: (0, 0))],
        out_specs=pl.BlockSpec((SGU_TM, BRANCH_W), lambda i: (i, 0)),
        scratch_shapes=[pltpu.VMEM((SGU_TM, BRANCH_W), BF16)],
        compiler_params=_params(("parallel",)),
        name="sgu",
    )(proj, proj, proj, ln_g, ws, bs_full)


_NT = (((1,), (1,)), ((), ()))
_TN = (((0,), (0,)), ((), ()))


def _split3_dot(mat, x):
    x1 = x.astype(BF16)
    r1 = x - x1.astype(F32)
    x2 = r1.astype(BF16)
    x3 = (r1 - x2.astype(F32)).astype(BF16)
    return (jnp.dot(mat, x1, preferred_element_type=F32)
            + jnp.dot(mat, x2, preferred_element_type=F32)
            + jnp.dot(mat, x3, preferred_element_type=F32))


def _gla_kernel(qf_ref, kf_ref, vf_ref, lrf_ref, qb_ref, kb_ref, vb_ref, lrb_ref, z_ref, wgk_ref,
                bgk_ref, ng_ref, o_ref, st_ref, part_ref, *, nb):
    j = pl.program_id(2)
    tb = GLA_TB
    nchunk = tb // GLA_CHUNK
    dirs = (0, 1)

    @pl.when(j == 0)
    def _():
        st_ref[...] = jnp.zeros_like(st_ref)

    ri = lax.broadcasted_iota(jnp.int32, (tb, tb), 0)
    ci = lax.broadcasted_iota(jnp.int32, (tb, tb), 1)
    same = (ri // GLA_CHUNK) == (ci // GLA_CHUNK)
    tris = (jnp.logical_and(same, ci <= ri), jnp.logical_and(same, ci >= ri))
    q_refs, k_refs, v_refs, lr_refs = (qf_ref, qb_ref), (kf_ref, kb_ref), (vf_ref, vb_ref), (lrf_ref, lrb_ref)
    rows = [slice(c * GLA_CHUNK, (c + 1) * GLA_CHUNK) for c in range(nchunk)]
    orders = (list(range(nchunk)), list(range(nchunk - 1, -1, -1)))
    last = (GLA_CHUNK - 1, 0)

    gs = []
    for d in dirs:
        x = jnp.dot(lr_refs[d][...], wgk_ref[d], preferred_element_type=F32) + bgk_ref[d]
        gs.append((jnp.minimum(x, 0.0) - jnp.log(1.0 + jnp.exp(-jnp.abs(x)))) * (1.0 / GLA_GATE_TEMP))
    gcs = [_split3_dot(jnp.where(tris[d], 1.0, 0.0).astype(BF16), gs[d]) for d in dirs]

    q_e, k_e, k_d, gl, vs = [], [], [], [], []
    for d in dirs:
        gc = gcs[d]
        gl_d = [gc[c * GLA_CHUNK + last[d]:c * GLA_CHUNK + last[d] + 1, :] for c in range(nchunk)]
        gt = jnp.concatenate([jnp.broadcast_to(t, (GLA_CHUNK, GLA_HEAD_K)) for t in gl_d], axis=0)
        q = q_refs[d][...].astype(F32) * (GLA_HEAD_K ** -0.5)
        k = k_refs[d][...].astype(F32)
        q_e.append((q * jnp.exp(gc)).astype(BF16))
        k_e.append((k * jnp.exp(-gc)).astype(BF16))
        k_d.append((k * jnp.exp(gt - gc)).astype(BF16))
        gl.append(gl_d)
        vs.append(v_refs[d][...])

    att = []
    for d in dirs:
        a = lax.dot_general(q_e[d], k_e[d], _NT, preferred_element_type=F32)
        att.append(jnp.where(tris[d], a, 0.0).astype(BF16))
    o_intra = [jnp.dot(att[d], vs[d], preferred_element_type=F32) for d in dirs]

    def upd(d, c):
        return lax.dot_general(vs[d][rows[c]], k_d[d][rows[c]], _TN, preferred_element_type=F32)

    pieces = [[None] * nchunk, [None] * nchunk]
    u_next = [upd(d, orders[d][0]) for d in dirs]
    for idx in range(nchunk):
        for d in dirs:
            c = orders[d][idx]
            u_cur = u_next[d]
            if idx + 1 < nchunk:
                u_next[d] = upd(d, orders[d][idx + 1])
            st = st_ref[d]
            pieces[d][c] = o_intra[d][rows[c]] + lax.dot_general(
                q_e[d][rows[c]], st.astype(BF16), _NT, preferred_element_type=F32)
            st_ref[d] = st * jnp.exp(gl[d][c]) + u_cur
    outs = [jnp.concatenate(pieces[d], axis=0) for d in dirs]
    toks = (pl.multiple_of(j * tb, tb), pl.multiple_of((nb - 1 - j) * tb, tb))

    @pl.when(j < nb // 2)
    def _():
        for d in dirs:
            part_ref[pl.ds(toks[d], tb), :] = outs[d]

    @pl.when(j >= nb // 2)
    def _():
        for d in dirs:
            tot = part_ref[pl.ds(toks[d], tb), :] + outs[d]
            ms = jnp.mean(tot * tot, axis=-1, keepdims=True)
            y = tot * lax.rsqrt(ms + EPS) * ng_ref[...]
            z = z_ref[pl.ds(toks[d], tb), :].astype(F32)
            o_ref[pl.ds(toks[d], tb), :] = (y * _silu(z)).astype(o_ref.dtype)


def _gla(proj, lr, wgk_pad, bgk, norm_g, batch, seq):
    m = proj.shape[0]
    nb = seq // GLA_TB
    assert nb % 2 == 0
    kq, kv = GLA_HEAD_K, GLA_HEAD_V

    def fwd(b, j):
        return b * nb + j

    def bwd(b, j):
        return b * nb + nb - 1 - j

    def blocks(tok):
        return [pl.BlockSpec((GLA_TB, kq), lambda b, h, j: (tok(b, j), COL_B_Q // kq + h)),
                pl.BlockSpec((GLA_TB, kq), lambda b, h, j: (tok(b, j), COL_B_K // kq + h)),
                pl.BlockSpec((GLA_TB, kv), lambda b, h, j: (tok(b, j), COL_B_V // kv + h)),
                pl.BlockSpec((GLA_TB, LR_PAD), lambda b, h, j: (tok(b, j), 0))]

    return pl.pallas_call(
        functools.partial(_gla_kernel, nb=nb),
        out_shape=jax.ShapeDtypeStruct((m, BRANCH_W), BF16),
        grid=(batch, GLA_HEADS, nb),
        in_specs=blocks(fwd) + blocks(bwd) + [
            pl.BlockSpec((seq, kv), lambda b, h, j: (b, COL_B_Z // kv + h)),
            pl.BlockSpec((2, LR_PAD, kq), lambda b, h, j: (0, 0, h)),
            pl.BlockSpec((2, 1, kq), lambda b, h, j: (0, 0, h)),
            pl.BlockSpec((1, kv), lambda b, h, j: (0, 0))],
        out_specs=pl.BlockSpec((seq, kv), lambda b, h, j: (b, h)),
        scratch_shapes=[pltpu.VMEM((2, kv, kq), F32), pltpu.VMEM((seq, kv), F32)],
        compiler_params=_params(("parallel", "parallel", "arbitrary")),
        name="gla",
    )(proj, proj, proj, lr, proj, proj, proj, lr, proj, wgk_pad, bgk, norm_g)


def _na_kernel(q_ref, k_ref, v_ref, z_ref, bias_ref, o_ref, *, rows):
    win = NA_ROWS * GRID_W
    scale = NA_HEAD_DIM ** -0.5

    def group(g, carry):
        q0s, k0s, scores = [], [], []
        for i in range(NA_GROUP):
            r = g * NA_GROUP + i
            rs = jnp.clip(r - NA_ROWS // 2, 0, rows - NA_ROWS)
            q0 = pl.multiple_of(r * GRID_W, GRID_W)
            k0 = pl.multiple_of(rs * GRID_W, GRID_W)
            s = lax.dot_general(q_ref[pl.ds(q0, GRID_W), :], k_ref[pl.ds(k0, win), :], _NT,
                                preferred_element_type=F32)
            scores.append(s * scale + bias_ref[0, r - rs])
            q0s.append(q0)
            k0s.append(k0)
        probs, inv = [], []
        for s in scores:
            p = jnp.exp(s - jnp.max(s, axis=-1, keepdims=True))
            inv.append(1.0 / jnp.sum(p, axis=-1, keepdims=True))
            probs.append(p.astype(BF16))
        for i in range(NA_GROUP):
            o = jnp.dot(probs[i], v_ref[pl.ds(k0s[i], win), :], preferred_element_type=F32) * inv[i]
            z = z_ref[pl.ds(q0s[i], GRID_W), :].astype(F32)
            o_ref[pl.ds(q0s[i], GRID_W), :] = (o * _silu(z)).astype(o_ref.dtype)
        return carry

    lax.fori_loop(0, rows // NA_GROUP, group, 0)


def _na(proj, bias_tab, batch, seq):
    m = proj.shape[0]
    rows = seq // GRID_W
    hd = NA_HEAD_DIM
    return pl.pallas_call(
        functools.partial(_na_kernel, rows=rows),
        out_shape=jax.ShapeDtypeStruct((m, BRANCH_W), BF16),
        grid=(NA_HEADS, batch),
        in_specs=[pl.BlockSpec((seq, hd), lambda h, b: (b, COL_C_Q // hd + h)),
                  pl.BlockSpec((seq, hd), lambda h, b: (b, COL_C_K // hd + h)),
                  pl.BlockSpec((seq, hd), lambda h, b: (b, COL_C_V // hd + h)),
                  pl.BlockSpec((seq, hd), lambda h, b: (b, COL_C_Z // hd + h)),
                  pl.BlockSpec((1, NA_ROWS, GRID_W, NA_ROWS * GRID_W), lambda h, b: (h, 0, 0, 0))],
        out_specs=pl.BlockSpec((seq, hd), lambda h, b: (b, h)),
        compiler_params=_params(("parallel", "parallel")),
        name="na",
    )(proj, proj, proj, proj, bias_tab)


def _merge_kernel(ya_ref, yb_ref, yc_ref, wb_ref, ga_ref, gb_ref, gc_ref, o_ref):
    y_refs = (ya_ref, yb_ref, yc_ref)
    g_refs = (ga_ref, gb_ref, gc_ref)
    for c in range(o_ref.shape[1] // PROJ_CHUNK):
        cols = slice(c * PROJ_CHUNK, (c + 1) * PROJ_CHUNK)
        acc = None
        for b in range(N_BRANCH):
            t = g_refs[b][:, cols].astype(F32) * jnp.dot(y_refs[b][...], wb_ref[b, :, cols],
                                                         preferred_element_type=F32)
            acc = t if acc is None else acc + t
        o_ref[:, cols] = acc.astype(o_ref.dtype)


def _merge(ya, yb, yc, wb, gates):
    m = ya.shape[0]
    tm, tn = MERGE_TM, MERGE_TN
    y_spec = pl.BlockSpec((tm, BRANCH_W), lambda i, j: (i, 0))

    def gate_spec(b):
        return pl.BlockSpec((tm, tn), lambda i, j: (i, b * D_MODEL // tn + j))

    return pl.pallas_call(
        _merge_kernel,
        out_shape=jax.ShapeDtypeStruct((m, D_MODEL), BF16),
        grid=(m // tm, D_MODEL // tn),
        in_specs=[y_spec, y_spec, y_spec,
                  pl.BlockSpec((N_BRANCH, BRANCH_W, tn), lambda i, j: (0, 0, j)),
                  gate_spec(0), gate_spec(1), gate_spec(2)],
        out_specs=pl.BlockSpec((tm, tn), lambda i, j: (i, j)),
        compiler_params=_params(("parallel", "arbitrary")),
        name="merge",
    )(ya, yb, yc, wb, gates, gates, gates)


def _out_kernel(m_ref, w_ref, x_ref, o_ref):
    _dot_chunks(m_ref, w_ref, o_ref, lambda acc, cols: x_ref[:, cols] + acc)


def _out(merged, wo, x):
    m = x.shape[0]
    tm, tn = OUT_TM, OUT_TN
    return pl.pallas_call(
        _out_kernel,
        out_shape=jax.ShapeDtypeStruct((m, D_MODEL), F32),
        grid=(m // tm, D_MODEL // tn),
        in_specs=[pl.BlockSpec((tm, D_MODEL), lambda i, j: (i, 0)),
                  pl.BlockSpec((D_MODEL, tn), lambda i, j: (0, j)),
                  pl.BlockSpec((tm, tn), lambda i, j: (i, j))],
        out_specs=pl.BlockSpec((tm, tn), lambda i, j: (i, j)),
        compiler_params=_params(("parallel", "arbitrary")),
        name="outproj",
    )(merged, wo, x)


def _na_bias_table(rpb):
    cols = jnp.arange(GRID_W)
    cs = jnp.clip(cols - NA_COLS // 2, 0, GRID_W - NA_COLS)
    col_ok = (cols[None, :] >= cs[:, None]) & (cols[None, :] < cs[:, None] + NA_COLS)
    dc = jnp.clip(cols[None, :] - cols[:, None] + NA_COLS - 1, 0, 2 * NA_COLS - 2)
    toe = jnp.zeros(rpb.shape[:3] + (GRID_W, GRID_W), F32)
    for c in range(2 * NA_COLS - 1):
        toe = jnp.where(dc == c, rpb[:, :, :, c, None, None], toe)
    toe = jnp.where(col_ok, toe, NEG_BIAS)
    tab = jnp.stack([toe[:, :, NA_ROWS - 1 - d:2 * NA_ROWS - 1 - d] for d in range(NA_ROWS)], axis=2)
    tab = tab.transpose(0, 1, 2, 4, 3, 5)
    return tab.reshape(rpb.shape[0], NA_HEADS, NA_ROWS, GRID_W, NA_ROWS * GRID_W)


def _trunk(x, batch, seq, p):
    h = _rmsnorm(x, p["norm_g"][0], BF16)
    for l in range(DEPTH):
        pab, lr = _proj(h, p["w_ab"][l], w_lr=p["w_lr"][l])
        pc = _proj(h, p["w_c"][l])
        gates = _proj(h, p["w_gate"][l], b_gate=p["b_gate"][l])
        ya = _sgu(pab, p["sgu_ln_g"][l], p["sgu_w"][l], p["sgu_b"][l])
        yb = _gla(pab, lr, p["wgk"][l], p["bgk"][l], p["gla_norm_g"][l], batch, seq)
        yc = _na(pc, p["na_bias"][l], batch, seq)
        merged = _merge(ya, yb, yc, p["w_branch"][l], gates)
        x = _out(merged, p["w_out"][l], x)
        if l + 1 < DEPTH:
            h = _rmsnorm(x, p["norm_g"][l + 1], BF16)
    return _rmsnorm(x, p["final_norm_g"], F32)


def kernel(x_prompt, x_sample, norm_g, w_in, sgu_ln_g, sgu_w, sgu_b, gla_w_gk, gla_b_gk, gla_norm_g,
           na_rpb, w_gate, b_gate, w_branch, w_out, final_norm_g):
    lr0 = AB_COLS
    lr1 = lr0 + 2 * GLA_LOW_RANK
    w_lr = jnp.pad(w_in[:, :, lr0:lr1], ((0, 0), (0, 0), (0, LR_PAD - 2 * GLA_LOW_RANK))).astype(BF16)
    wgk = jnp.zeros((DEPTH, 2, LR_PAD, GLA_KEY_W), F32)
    wgk = wgk.at[:, 0, :GLA_LOW_RANK].set(gla_w_gk[:, 0])
    wgk = wgk.at[:, 1, GLA_LOW_RANK:2 * GLA_LOW_RANK].set(gla_w_gk[:, 1])
    p = {
        "norm_g": norm_g,
        "final_norm_g": final_norm_g,
        "w_ab": w_in[:, :, :lr0].astype(BF16),
        "w_c": w_in[:, :, lr1:].astype(BF16),
        "w_gate": w_gate.astype(BF16),
        "w_lr": w_lr,
        "b_gate": b_gate.reshape(DEPTH, 1, N_BRANCH * D_MODEL),
        "sgu_ln_g": sgu_ln_g.reshape(DEPTH, 1, BRANCH_W),
        "sgu_w": sgu_w.astype(BF16),
        "sgu_b": jnp.repeat(jnp.swapaxes(sgu_b, 1, 2), SGU_GROUP_W, axis=2),
        "wgk": wgk.astype(BF16),
        "bgk": gla_b_gk.reshape(DEPTH, 2, 1, GLA_KEY_W),
        "gla_norm_g": gla_norm_g.reshape(DEPTH, 1, GLA_HEAD_V),
        "na_bias": _na_bias_table(na_rpb),
        "w_branch": w_branch.astype(BF16),
        "w_out": w_out.astype(BF16),
    }
    bp, tp, _ = x_prompt.shape
    bs, ts, _ = x_sample.shape
    y_prompt = _trunk(x_prompt.reshape(bp * tp, D_MODEL), bp, tp, p).reshape(x_prompt.shape)
    y_sample = _trunk(x_sample.reshape(bs * ts, D_MODEL), bs, ts, p).reshape(x_sample.shape)
    return (y_prompt, y_sample)
```

```python
import functools

import jax
import jax.numpy as jnp
from jax import lax
from jax.experimental import pallas as pl
from jax.experimental.pallas import tpu as pltpu

F32 = jnp.float32
BF16 = jnp.bfloat16

D_MODEL = 4096
DEPTH = 4
BRANCH_W = D_MODEL // 2
N_BRANCH = 3
GRID_W = 64
SGU_CHUNK = 128
SGU_GROUPS = 8
SGU_GROUP_W = BRANCH_W // SGU_GROUPS
GLA_HEADS = 4
GLA_KEY_W = BRANCH_W // 2
GLA_HEAD_K = GLA_KEY_W // GLA_HEADS
GLA_HEAD_V = BRANCH_W // GLA_HEADS
GLA_LOW_RANK = 16
GLA_GATE_TEMP = 16.0
GLA_CHUNK = 64
NA_HEAD_DIM = 128
NA_HEADS = BRANCH_W // NA_HEAD_DIM
NA_ROWS = 8
NA_COLS = 16
EPS = 1e-6
LN2 = 0.6931471805599453

COL_A_U = 0
COL_A_V = COL_A_U + BRANCH_W
COL_A_Z = COL_A_V + BRANCH_W
COL_B_Q = COL_A_Z + BRANCH_W
COL_B_K = COL_B_Q + GLA_KEY_W
COL_B_V = COL_B_K + GLA_KEY_W
COL_B_Z = COL_B_V + BRANCH_W
AB_COLS = COL_B_Z + BRANCH_W
COL_C_Q = 0
COL_C_K = COL_C_Q + BRANCH_W
COL_C_V = COL_C_K + BRANCH_W
COL_C_Z = COL_C_V + BRANCH_W
C_COLS = COL_C_Z + BRANCH_W
LR_PAD = 128
STAT_LANES = 128

V7X_LANES = 128
VMEM_LIMIT = 56 * 1024 * 1024
NEG_BIAS = -1e30

PROJ_TM, PROJ_TN = 1024, 1024
PROJ_CHUNK = 256
NORM_TM = 256
SGU_TM = 512
GLA_TB = 256
GLA_SKEW = 0
NA_GROUP = 8
MERGE_TM, MERGE_TN = 1024, 512
OUT_TM, OUT_TN = 1024, 1024


def _params(sem):
    return pltpu.CompilerParams(dimension_semantics=sem, vmem_limit_bytes=VMEM_LIMIT)


def _silu(z):
    return z * jax.nn.sigmoid(z)


def _rmsnorm_kernel(x_ref, g_ref, o_ref):
    x = x_ref[...]
    ms = jnp.mean(x * x, axis=-1, keepdims=True)
    o_ref[...] = (x * lax.rsqrt(ms + EPS) * g_ref[...]).astype(o_ref.dtype)


def _rmsnorm(x, g, out_dtype):
    m = x.shape[0]
    return pl.pallas_call(
        _rmsnorm_kernel,
        out_shape=jax.ShapeDtypeStruct((m, D_MODEL), out_dtype),
        grid=(m // NORM_TM,),
        in_specs=[pl.BlockSpec((NORM_TM, D_MODEL), lambda i: (i, 0)),
                  pl.BlockSpec((1, D_MODEL), lambda i: (0, 0))],
        out_specs=pl.BlockSpec((NORM_TM, D_MODEL), lambda i: (i, 0)),
        compiler_params=_params(("parallel",)),
        name="rmsnorm",
    )(x, g.reshape(1, D_MODEL))


def _prenorm_kernel(x_ref, g_ref, xg_ref, ssq_ref):
    x = x_ref[...]
    xg_ref[...] = (x * g_ref[...]).astype(xg_ref.dtype)
    ssq_ref[...] = jnp.broadcast_to(jnp.sum(x * x, axis=-1, keepdims=True), ssq_ref.shape)


def _prenorm(x, g):
    m = x.shape[0]
    return pl.pallas_call(
        _prenorm_kernel,
        out_shape=(jax.ShapeDtypeStruct((m, D_MODEL), BF16), jax.ShapeDtypeStruct((m, STAT_LANES), F32)),
        grid=(m // NORM_TM,),
        in_specs=[pl.BlockSpec((NORM_TM, D_MODEL), lambda i: (i, 0)),
                  pl.BlockSpec((1, D_MODEL), lambda i: (0, 0))],
        out_specs=(pl.BlockSpec((NORM_TM, D_MODEL), lambda i: (i, 0)),
                   pl.BlockSpec((NORM_TM, STAT_LANES), lambda i: (i, 0))),
        compiler_params=_params(("parallel",)),
        name="prenorm",
    )(x, g.reshape(1, D_MODEL))


def _rstd(ssq_ref):
    return lax.rsqrt(ssq_ref[...] * (1.0 / D_MODEL) + EPS)


def _dot_chunks(h_ref, w_ref, o_ref, epilogue):
    for c in range(o_ref.shape[1] // PROJ_CHUNK):
        cols = slice(c * PROJ_CHUNK, (c + 1) * PROJ_CHUNK)
        acc = jnp.dot(h_ref[...], w_ref[:, cols], preferred_element_type=F32)
        o_ref[:, cols] = epilogue(acc, cols).astype(o_ref.dtype)


def _proj_kernel(xg_ref, ssq_ref, w_ref, o_ref):
    rstd = jnp.concatenate([_rstd(ssq_ref)] * (PROJ_CHUNK // STAT_LANES), axis=1)
    _dot_chunks(xg_ref, w_ref, o_ref, lambda acc, cols: acc * rstd)


def _proj_lr_kernel(xg_ref, ssq_ref, w_ref, wlr_ref, o_ref, olr_ref):
    _proj_kernel(xg_ref, ssq_ref, w_ref, o_ref)

    @pl.when(pl.program_id(1) == 0)
    def _():
        lr = jnp.dot(xg_ref[...], wlr_ref[...], preferred_element_type=F32) * _rstd(ssq_ref)
        olr_ref[...] = lr.astype(olr_ref.dtype)


def _proj_gate_kernel(xg_ref, ssq_ref, w_ref, b_ref, o_ref):
    rstd = jnp.concatenate([_rstd(ssq_ref)] * (PROJ_CHUNK // STAT_LANES), axis=1)
    _dot_chunks(xg_ref, w_ref, o_ref, lambda acc, cols: jax.nn.sigmoid(acc * rstd + b_ref[:, cols]))


def _proj(xg, ssq, w, w_lr=None, b_gate=None):
    m = xg.shape[0]
    n = w.shape[1]
    assert LR_PAD == STAT_LANES
    h_spec = pl.BlockSpec((PROJ_TM, D_MODEL), lambda i, j: (i, 0))
    s_spec = pl.BlockSpec((PROJ_TM, STAT_LANES), lambda i, j: (i, 0))
    w_spec = pl.BlockSpec((D_MODEL, PROJ_TN), lambda i, j: (0, j))
    o_spec = pl.BlockSpec((PROJ_TM, PROJ_TN), lambda i, j: (i, j))
    o_shape = jax.ShapeDtypeStruct((m, n), BF16)
    common = dict(grid=(m // PROJ_TM, n // PROJ_TN), compiler_params=_params(("parallel", "arbitrary")))
    if w_lr is not None:
        return pl.pallas_call(
            _proj_lr_kernel,
            out_shape=(o_shape, jax.ShapeDtypeStruct((m, LR_PAD), BF16)),
            in_specs=[h_spec, s_spec, w_spec, pl.BlockSpec((D_MODEL, LR_PAD), lambda i, j: (0, 0))],
            out_specs=(o_spec, pl.BlockSpec((PROJ_TM, LR_PAD), lambda i, j: (i, 0))),
            name="proj_ab", **common)(xg, ssq, w, w_lr)
    if b_gate is not None:
        return pl.pallas_call(
            _proj_gate_kernel, out_shape=o_shape,
            in_specs=[h_spec, s_spec, w_spec, pl.BlockSpec((1, PROJ_TN), lambda i, j: (0, j))],
            out_specs=o_spec, name="proj_gate", **common)(xg, ssq, w, b_gate)
    return pl.pallas_call(
        _proj_kernel, out_shape=o_shape, in_specs=[h_spec, s_spec, w_spec], out_specs=o_spec,
        name="proj_c", **common)(xg, ssq, w)


def _sgu_kernel(u_ref, v_ref, z_ref, lng_ref, ws_ref, bs_ref, o_ref, vn_ref):
    v = v_ref[...].astype(F32)
    mu = jnp.mean(v, axis=-1, keepdims=True)
    xc = v - mu
    var = jnp.mean(xc * xc, axis=-1, keepdims=True)
    vn_ref[...] = (xc * lax.rsqrt(var + EPS) * lng_ref[...]).astype(vn_ref.dtype)
    for c in range(SGU_TM // SGU_CHUNK):
        rows = slice(c * SGU_CHUNK, (c + 1) * SGU_CHUNK)
        for g in range(SGU_GROUPS):
            cols = slice(g * SGU_GROUP_W, (g + 1) * SGU_GROUP_W)
            s = jnp.dot(ws_ref[g], vn_ref[rows, cols], preferred_element_type=F32) + bs_ref[:, cols]
            u = u_ref[rows, cols].astype(F32)
            z = z_ref[rows, cols].astype(F32)
            o_ref[rows, cols] = (u * s * _silu(z)).astype(o_ref.dtype)


def _sgu(proj, ln_g, ws, bs_full):
    m = proj.shape[0]
    return pl.pallas_call(
        _sgu_kernel,
        out_shape=jax.ShapeDtypeStruct((m, BRANCH_W), BF16),
        grid=(m // SGU_TM,),
        in_specs=[pl.BlockSpec((SGU_TM, BRANCH_W), lambda i: (i, COL_A_U // BRANCH_W)),
                  pl.BlockSpec((SGU_TM, BRANCH_W), lambda i: (i, COL_A_V // BRANCH_W)),
                  pl.BlockSpec((SGU_TM, BRANCH_W), lambda i: (i, COL_A_Z // BRANCH_W)),
                  pl.BlockSpec((1, BRANCH_W), lambda i: (0, 0)),
                  pl.BlockSpec((SGU_GROUPS, SGU_CHUNK, SGU_CHUNK), lambda i: (0, 0, 0)),
                  pl.BlockSpec((SGU_CHUNK, BRANCH_W), lambda i: (0, 0))],
        out_specs=pl.BlockSpec((SGU_TM, BRANCH_W), lambda i: (i, 0)),
        scratch_shapes=[pltpu.VMEM((SGU_TM, BRANCH_W), BF16)],
        compiler_params=_params(("parallel",)),
        name="sgu",
    )(proj, proj, proj, ln_g, ws, bs_full)


_NT = (((1,), (1,)), ((), ()))
_TN = (((0,), (0,)), ((), ()))


def _split3_dot(mat, x):
    x1 = x.astype(BF16)
    r1 = x - x1.astype(F32)
    x2 = r1.astype(BF16)
    x3 = (r1 - x2.astype(F32)).astype(BF16)
    return (jnp.dot(mat, x1, preferred_element_type=F32)
            + jnp.dot(mat, x2, preferred_element_type=F32)
            + jnp.dot(mat, x3, preferred_element_type=F32))


def _gla_masks():
    ri = jnp.arange(GLA_TB)[:, None]
    ci = jnp.arange(GLA_TB)[None, :]
    same = (ri // GLA_CHUNK) == (ci // GLA_CHUNK)
    return jnp.stack([same & (ci <= ri), same & (ci >= ri)]).astype(F32)


def _gla_kernel(qf_ref, kf_ref, vf_ref, lrf_ref, qb_ref, kb_ref, vb_ref, lrb_ref, z_ref, wgk_ref,
                bgk_ref, ng_ref, tri_ref, trib_ref, o_ref, st_ref, part_ref, *, nb):
    j = pl.program_id(2)
    tb = GLA_TB
    nchunk = tb // GLA_CHUNK
    dirs = (0, 1)

    @pl.when(j == 0)
    def _():
        st_ref[...] = jnp.zeros_like(st_ref)

    q_refs, k_refs, v_refs, lr_refs = (qf_ref, qb_ref), (kf_ref, kb_ref), (vf_ref, vb_ref), (lrf_ref, lrb_ref)
    rows = [slice(c * GLA_CHUNK, (c + 1) * GLA_CHUNK) for c in range(nchunk)]
    orders = (list(range(nchunk)), list(range(nchunk - 1, -1, -1)))
    last = (GLA_CHUNK - 1, 0)

    s = [dict(pieces=[None] * nchunk), dict(pieces=[None] * nchunk)]

    def gate(d):
        x = jnp.dot(lr_refs[d][...], wgk_ref[d], preferred_element_type=F32) + bgk_ref[d]
        soft = jnp.log2(1.0 + jnp.exp(-jnp.abs(x)))
        s[d]["g"] = jnp.minimum(x, 0.0) * (1.0 / GLA_GATE_TEMP) - soft * (LN2 / GLA_GATE_TEMP)

    def cumsum(d):
        s[d]["gc"] = _split3_dot(trib_ref[d], s[d]["g"])

    def decays(d):
        gc = s[d]["gc"]
        gl = [gc[c * GLA_CHUNK + last[d]:c * GLA_CHUNK + last[d] + 1, :] for c in range(nchunk)]
        gt = jnp.concatenate([jnp.broadcast_to(t, (GLA_CHUNK, GLA_HEAD_K)) for t in gl], axis=0)
        q = q_refs[d][...].astype(F32) * (GLA_HEAD_K ** -0.5)
        k = k_refs[d][...].astype(F32)
        s[d].update(gl=gl, v=v_refs[d][...],
                    q_e=(q * jnp.exp(gc)).astype(BF16),
                    k_e=(k * jnp.exp(-gc)).astype(BF16),
                    k_d=(k * jnp.exp(gt - gc)).astype(BF16))

    def scores(d):
        a = lax.dot_general(s[d]["q_e"], s[d]["k_e"], _NT, preferred_element_type=F32)
        s[d]["att"] = jnp.where(tri_ref[d] > 0.5, a, 0.0).astype(BF16)

    def intra(d):
        s[d]["o_intra"] = jnp.dot(s[d]["att"], s[d]["v"], preferred_element_type=F32)

    def update(d, idx):
        c = orders[d][idx]
        s[d]["u"] = lax.dot_general(s[d]["v"][rows[c]], s[d]["k_d"][rows[c]], _TN,
                                    preferred_element_type=F32)

    def chunk(d, idx):
        c = orders[d][idx]
        u_cur = s[d]["u"]
        if idx + 1 < nchunk:
            update(d, idx + 1)
        st = st_ref[d]
        s[d]["pieces"][c] = s[d]["o_intra"][rows[c]] + lax.dot_general(
            s[d]["q_e"][rows[c]], st.astype(BF16), _NT, preferred_element_type=F32)
        st_ref[d] = st * jnp.exp(s[d]["gl"][c]) + u_cur

    stages = [gate, cumsum, decays, scores, intra, lambda d: update(d, 0)]
    stages += [functools.partial(lambda idx, d: chunk(d, idx), idx) for idx in range(nchunk)]
    for t in range(len(stages) + GLA_SKEW):
        if t < len(stages):
            stages[t](0)
        if 0 <= t - GLA_SKEW < len(stages):
            stages[t - GLA_SKEW](1)
    outs = [jnp.concatenate(s[d]["pieces"], axis=0) for d in dirs]
    toks = (pl.multiple_of(j * tb, tb), pl.multiple_of((nb - 1 - j) * tb, tb))

    @pl.when(j < nb // 2)
    def _():
        for d in dirs:
            part_ref[pl.ds(toks[d], tb), :] = outs[d]

    @pl.when(j >= nb // 2)
    def _():
        for d in dirs:
            tot = part_ref[pl.ds(toks[d], tb), :] + outs[d]
            ms = jnp.mean(tot * tot, axis=-1, keepdims=True)
            y = tot * lax.rsqrt(ms + EPS) * ng_ref[...]
            z = z_ref[pl.ds(toks[d], tb), :].astype(F32)
            o_ref[pl.ds(toks[d], tb), :] = (y * _silu(z)).astype(o_ref.dtype)


def _gla(proj, lr, wgk_pad, bgk, norm_g, batch, seq):
    m = proj.shape[0]
    nb = seq // GLA_TB
    assert nb % 2 == 0
    kq, kv = GLA_HEAD_K, GLA_HEAD_V
    masks = _gla_masks()

    def fwd(b, j):
        return b * nb + j

    def bwd(b, j):
        return b * nb + nb - 1 - j

    def blocks(tok):
        return [pl.BlockSpec((GLA_TB, kq), lambda b, h, j: (tok(b, j), COL_B_Q // kq + h)),
                pl.BlockSpec((GLA_TB, kq), lambda b, h, j: (tok(b, j), COL_B_K // kq + h)),
                pl.BlockSpec((GLA_TB, kv), lambda b, h, j: (tok(b, j), COL_B_V // kv + h)),
                pl.BlockSpec((GLA_TB, LR_PAD), lambda b, h, j: (tok(b, j), 0))]

    return pl.pallas_call(
        functools.partial(_gla_kernel, nb=nb),
        out_shape=jax.ShapeDtypeStruct((m, BRANCH_W), BF16),
        grid=(batch, GLA_HEADS, nb),
        in_specs=blocks(fwd) + blocks(bwd) + [
            pl.BlockSpec((seq, kv), lambda b, h, j: (b, COL_B_Z // kv + h)),
            pl.BlockSpec((2, LR_PAD, kq), lambda b, h, j: (0, 0, h)),
            pl.BlockSpec((2, 1, kq), lambda b, h, j: (0, 0, h)),
            pl.BlockSpec((1, kv), lambda b, h, j: (0, 0)),
            pl.BlockSpec((2, GLA_TB, GLA_TB), lambda b, h, j: (0, 0, 0)),
            pl.BlockSpec((2, GLA_TB, GLA_TB), lambda b, h, j: (0, 0, 0))],
        out_specs=pl.BlockSpec((seq, kv), lambda b, h, j: (b, h)),
        scratch_shapes=[pltpu.VMEM((2, kv, kq), F32), pltpu.VMEM((seq, kv), F32)],
        compiler_params=_params(("parallel", "parallel", "arbitrary")),
        name="gla",
    )(proj, proj, proj, lr, proj, proj, proj, lr, proj, wgk_pad, bgk, norm_g, masks, masks.astype(BF16))


def _na_kernel(q_ref, k_ref, v_ref, z_ref, bias_ref, o_ref, *, rows):
    win = NA_ROWS * GRID_W
    scale = NA_HEAD_DIM ** -0.5

    def group(g, carry):
        q0s, k0s, scores = [], [], []
        for i in range(NA_GROUP):
            r = g * NA_GROUP + i
            rs = jnp.clip(r - NA_ROWS // 2, 0, rows - NA_ROWS)
            q0 = pl.multiple_of(r * GRID_W, GRID_W)
            k0 = pl.multiple_of(rs * GRID_W, GRID_W)
            s = lax.dot_general(q_ref[pl.ds(q0, GRID_W), :], k_ref[pl.ds(k0, win), :], _NT,
                                preferred_element_type=F32)
            scores.append(s * scale + bias_ref[0, r - rs])
            q0s.append(q0)
            k0s.append(k0)
        probs, inv = [], []
        for s in scores:
            p = jnp.exp(s - jnp.max(s, axis=-1, keepdims=True))
            inv.append(1.0 / jnp.sum(p, axis=-1, keepdims=True))
            probs.append(p.astype(BF16))
        for i in range(NA_GROUP):
            o = jnp.dot(probs[i], v_ref[pl.ds(k0s[i], win), :], preferred_element_type=F32) * inv[i]
            z = z_ref[pl.ds(q0s[i], GRID_W), :].astype(F32)
            o_ref[pl.ds(q0s[i], GRID_W), :] = (o * _silu(z)).astype(o_ref.dtype)
        return carry

    lax.fori_loop(0, rows // NA_GROUP, group, 0)


def _na(proj, bias_tab, batch, seq):
    m = proj.shape[0]
    rows = seq // GRID_W
    hd = NA_HEAD_DIM
    return pl.pallas_call(
        functools.partial(_na_kernel, rows=rows),
        out_shape=jax.ShapeDtypeStruct((m, BRANCH_W), BF16),
        grid=(NA_HEADS, batch),
        in_specs=[pl.BlockSpec((seq, hd), lambda h, b: (b, COL_C_Q // hd + h)),
                  pl.BlockSpec((seq, hd), lambda h, b: (b, COL_C_K // hd + h)),
                  pl.BlockSpec((seq, hd), lambda h, b: (b, COL_C_V // hd + h)),
                  pl.BlockSpec((seq, hd), lambda h, b: (b, COL_C_Z // hd + h)),
                  pl.BlockSpec((1, NA_ROWS, GRID_W, NA_ROWS * GRID_W), lambda h, b: (h, 0, 0, 0))],
        out_specs=pl.BlockSpec((seq, hd), lambda h, b: (b, h)),
        compiler_params=_params(("parallel", "parallel")),
        name="na",
    )(proj, proj, proj, proj, bias_tab)


def _merge_kernel(ya_ref, yb_ref, yc_ref, wb_ref, ga_ref, gb_ref, gc_ref, o_ref):
    y_refs = (ya_ref, yb_ref, yc_ref)
    g_refs = (ga_ref, gb_ref, gc_ref)
    for c in range(o_ref.shape[1] // PROJ_CHUNK):
        cols = slice(c * PROJ_CHUNK, (c + 1) * PROJ_CHUNK)
        acc = None
        for b in range(N_BRANCH):
            t = g_refs[b][:, cols].astype(F32) * jnp.dot(y_refs[b][...], wb_ref[b, :, cols],
                                                         preferred_element_type=F32)
            acc = t if acc is None else acc + t
        o_ref[:, cols] = acc.astype(o_ref.dtype)


def _merge(ya, yb, yc, wb, gates):
    m = ya.shape[0]
    tm, tn = MERGE_TM, MERGE_TN
    y_spec = pl.BlockSpec((tm, BRANCH_W), lambda i, j: (i, 0))

    def gate_spec(b):
        return pl.BlockSpec((tm, tn), lambda i, j: (i, b * D_MODEL // tn + j))

    return pl.pallas_call(
        _merge_kernel,
        out_shape=jax.ShapeDtypeStruct((m, D_MODEL), BF16),
        grid=(m // tm, D_MODEL // tn),
        in_specs=[y_spec, y_spec, y_spec,
                  pl.BlockSpec((N_BRANCH, BRANCH_W, tn), lambda i, j: (0, 0, j)),
                  gate_spec(0), gate_spec(1), gate_spec(2)],
        out_specs=pl.BlockSpec((tm, tn), lambda i, j: (i, j)),
        compiler_params=_params(("parallel", "arbitrary")),
        name="merge",
    )(ya, yb, yc, wb, gates, gates, gates)


def _out_kernel(m_ref, w_ref, x_ref, o_ref):
    _dot_chunks(m_ref, w_ref, o_ref, lambda acc, cols: x_ref[:, cols] + acc)


def _out_prenorm_kernel(m_ref, w_ref, x_ref, g_ref, o_ref, xg_ref, ssq_ref):
    @pl.when(pl.program_id(1) == 0)
    def _():
        ssq_ref[...] = jnp.zeros_like(ssq_ref)

    ssq = None
    for c in range(o_ref.shape[1] // PROJ_CHUNK):
        cols = slice(c * PROJ_CHUNK, (c + 1) * PROJ_CHUNK)
        xn = x_ref[:, cols] + jnp.dot(m_ref[...], w_ref[:, cols], preferred_element_type=F32)
        o_ref[:, cols] = xn
        xg_ref[:, cols] = (xn * g_ref[:, cols]).astype(xg_ref.dtype)
        s = jnp.sum(xn * xn, axis=-1, keepdims=True)
        ssq = s if ssq is None else ssq + s
    ssq_ref[...] += jnp.broadcast_to(ssq, ssq_ref.shape)


def _out(merged, wo, x, g_next=None):
    m = x.shape[0]
    tm, tn = OUT_TM, OUT_TN
    in_specs = [pl.BlockSpec((tm, D_MODEL), lambda i, j: (i, 0)),
                pl.BlockSpec((D_MODEL, tn), lambda i, j: (0, j)),
                pl.BlockSpec((tm, tn), lambda i, j: (i, j))]
    x_spec = pl.BlockSpec((tm, tn), lambda i, j: (i, j))
    x_shape = jax.ShapeDtypeStruct((m, D_MODEL), F32)
    common = dict(grid=(m // tm, D_MODEL // tn), compiler_params=_params(("parallel", "arbitrary")))
    if g_next is None:
        return pl.pallas_call(_out_kernel, out_shape=x_shape, in_specs=in_specs, out_specs=x_spec,
                              name="outproj", **common)(merged, wo, x)
    return pl.pallas_call(
        _out_prenorm_kernel,
        out_shape=(x_shape, jax.ShapeDtypeStruct((m, D_MODEL), BF16),
                   jax.ShapeDtypeStruct((m, STAT_LANES), F32)),
        in_specs=in_specs + [pl.BlockSpec((1, tn), lambda i, j: (0, j))],
        out_specs=(x_spec, pl.BlockSpec((tm, tn), lambda i, j: (i, j)),
                   pl.BlockSpec((tm, STAT_LANES), lambda i, j: (i, 0))),
        name="outproj_prenorm", **common)(merged, wo, x, g_next.reshape(1, D_MODEL))


def _na_bias_table(rpb):
    cols = jnp.arange(GRID_W)
    cs = jnp.clip(cols - NA_COLS // 2, 0, GRID_W - NA_COLS)
    col_ok = (cols[None, :] >= cs[:, None]) & (cols[None, :] < cs[:, None] + NA_COLS)
    dc = jnp.clip(cols[None, :] - cols[:, None] + NA_COLS - 1, 0, 2 * NA_COLS - 2)
    toe = jnp.zeros(rpb.shape[:3] + (GRID_W, GRID_W), F32)
    for c in range(2 * NA_COLS - 1):
        toe = jnp.where(dc == c, rpb[:, :, :, c, None, None], toe)
    toe = jnp.where(col_ok, toe, NEG_BIAS)
    return jnp.stack(
        [jnp.concatenate([toe[:, :, n - d + NA_ROWS - 1] for n in range(NA_ROWS)], axis=-1)
         for d in range(NA_ROWS)], axis=2)


def _trunk(x, batch, seq, p):
    xg, ssq = _prenorm(x, p["norm_g"][0])
    for l in range(DEPTH):
        pab, lr = _proj(xg, ssq, p["w_ab"][l], w_lr=p["w_lr"][l])
        pc = _proj(xg, ssq, p["w_c"][l])
        gates = _proj(xg, ssq, p["w_gate"][l], b_gate=p["b_gate"][l])
        ya = _sgu(pab, p["sgu_ln_g"][l], p["sgu_w"][l], p["sgu_b"][l])
        yb = _gla(pab, lr, p["wgk"][l], p["bgk"][l], p["gla_norm_g"][l], batch, seq)
        yc = _na(pc, p["na_bias"][l], batch, seq)
        merged = _merge(ya, yb, yc, p["w_branch"][l], gates)
        if l + 1 < DEPTH:
            x, xg, ssq = _out(merged, p["w_out"][l], x, g_next=p["norm_g"][l + 1])
        else:
            x = _out(merged, p["w_out"][l], x)
    return _rmsnorm(x, p["final_norm_g"], F32)


def kernel(x_prompt, x_sample, norm_g, w_in, sgu_ln_g, sgu_w, sgu_b, gla_w_gk, gla_b_gk, gla_norm_g,
           na_rpb, w_gate, b_gate, w_branch, w_out, final_norm_g):
    lr0 = AB_COLS
    lr1 = lr0 + 2 * GLA_LOW_RANK
    w_lr = jnp.pad(w_in[:, :, lr0:lr1], ((0, 0), (0, 0), (0, LR_PAD - 2 * GLA_LOW_RANK))).astype(BF16)
    wgk = jnp.zeros((DEPTH, 2, LR_PAD, GLA_KEY_W), F32)
    wgk = wgk.at[:, 0, :GLA_LOW_RANK].set(gla_w_gk[:, 0])
    wgk = wgk.at[:, 1, GLA_LOW_RANK:2 * GLA_LOW_RANK].set(gla_w_gk[:, 1])
    p = {
        "norm_g": norm_g,
        "final_norm_g": final_norm_g,
        "w_ab": w_in[:, :, :lr0].astype(BF16),
        "w_c": w_in[:, :, lr1:].astype(BF16),
        "w_gate": w_gate.astype(BF16),
        "w_lr": w_lr,
        "b_gate": b_gate.reshape(DEPTH, 1, N_BRANCH * D_MODEL),
        "sgu_ln_g": sgu_ln_g.reshape(DEPTH, 1, BRANCH_W),
        "sgu_w": sgu_w.astype(BF16),
        "sgu_b": jnp.repeat(jnp.swapaxes(sgu_b, 1, 2), SGU_GROUP_W, axis=2),
        "wgk": wgk.astype(BF16),
        "bgk": gla_b_gk.reshape(DEPTH, 2, 1, GLA_KEY_W),
        "gla_norm_g": gla_norm_g.reshape(DEPTH, 1, GLA_HEAD_V),
        "na_bias": _na_bias_table(na_rpb),
        "w_branch": w_branch.astype(BF16),
        "w_out": w_out.astype(BF16),
    }
    bp, tp, _ = x_prompt.shape
    bs, ts, _ = x_sample.shape
    y_prompt = _trunk(x_prompt.reshape(bp * tp, D_MODEL), bp, tp, p).reshape(x_prompt.shape)
    y_sample = _trunk(x_sample.reshape(bs * ts, D_MODEL), bs, ts, p).reshape(x_sample.shape)
    return (y_prompt, y_sample)
```

```python
import functools

import jax
import jax.numpy as jnp
from jax import lax
from jax.experimental import pallas as pl
from jax.experimental.pallas import tpu as pltpu

F32 = jnp.float32
BF16 = jnp.bfloat16

D_MODEL = 4096
DEPTH = 4
BRANCH_W = D_MODEL // 2
N_BRANCH = 3
GRID_W = 64
SGU_CHUNK = 128
SGU_GROUPS = 8
SGU_GROUP_W = BRANCH_W // SGU_GROUPS
GLA_HEADS = 4
GLA_KEY_W = BRANCH_W // 2
GLA_HEAD_K = GLA_KEY_W // GLA_HEADS
GLA_HEAD_V = BRANCH_W // GLA_HEADS
GLA_LOW_RANK = 16
GLA_GATE_TEMP = 16.0
GLA_CHUNK = 64
NA_HEAD_DIM = 128
NA_HEADS = BRANCH_W // NA_HEAD_DIM
NA_ROWS = 8
NA_COLS = 16
EPS = 1e-6
LOG2E = 1.4426950408889634

COL_A_U = 0
COL_A_V = COL_A_U + BRANCH_W
COL_A_Z = COL_A_V + BRANCH_W
COL_B_Q = COL_A_Z + BRANCH_W
COL_B_K = COL_B_Q + GLA_KEY_W
COL_B_V = COL_B_K + GLA_KEY_W
COL_B_Z = COL_B_V + BRANCH_W
AB_COLS = COL_B_Z + BRANCH_W
COL_C_Q = 0
COL_C_K = COL_C_Q + BRANCH_W
COL_C_V = COL_C_K + BRANCH_W
COL_C_Z = COL_C_V + BRANCH_W
C_COLS = COL_C_Z + BRANCH_W
LR_PAD = 128
STAT_LANES = 128

V7X_LANES = 128
VMEM_LIMIT = 56 * 1024 * 1024
NEG_BIAS = -1e30

PROJ_TM, PROJ_TN = 1024, 1024
PROJ_CHUNK = 256
NORM_TM = 256
CAST_TR, CAST_TC = 512, 1024
SGU_TM = 512
GLA_TB = 256
GLA_SKEW = 0
NA_GROUP = 32
MERGE_TM, MERGE_TN = 1024, 512
OUT_TM, OUT_TN = 1024, 1024


def _params(sem):
    return pltpu.CompilerParams(dimension_semantics=sem, vmem_limit_bytes=VMEM_LIMIT)


def _silu(z):
    return z * jax.nn.sigmoid(z)


def _rmsnorm_kernel(x_ref, g_ref, o_ref):
    x = x_ref[...]
    ms = jnp.mean(x * x, axis=-1, keepdims=True)
    o_ref[...] = (x * lax.rsqrt(ms + EPS) * g_ref[...]).astype(o_ref.dtype)


def _rmsnorm(x, g, out_dtype):
    m = x.shape[0]
    return pl.pallas_call(
        _rmsnorm_kernel,
        out_shape=jax.ShapeDtypeStruct((m, D_MODEL), out_dtype),
        grid=(m // NORM_TM,),
        in_specs=[pl.BlockSpec((NORM_TM, D_MODEL), lambda i: (i, 0)),
                  pl.BlockSpec((1, D_MODEL), lambda i: (0, 0))],
        out_specs=pl.BlockSpec((NORM_TM, D_MODEL), lambda i: (i, 0)),
        compiler_params=_params(("parallel",)),
        name="rmsnorm",
    )(x, g.reshape(1, D_MODEL))


def _prenorm_kernel(x_ref, g_ref, xg_ref, ssq_ref):
    x = x_ref[...]
    xg_ref[...] = (x * g_ref[...]).astype(xg_ref.dtype)
    ssq_ref[...] = jnp.broadcast_to(jnp.sum(x * x, axis=-1, keepdims=True), ssq_ref.shape)


def _prenorm(x, g):
    m = x.shape[0]
    return pl.pallas_call(
        _prenorm_kernel,
        out_shape=(jax.ShapeDtypeStruct((m, D_MODEL), BF16), jax.ShapeDtypeStruct((m, STAT_LANES), F32)),
        grid=(m // NORM_TM,),
        in_specs=[pl.BlockSpec((NORM_TM, D_MODEL), lambda i: (i, 0)),
                  pl.BlockSpec((1, D_MODEL), lambda i: (0, 0))],
        out_specs=(pl.BlockSpec((NORM_TM, D_MODEL), lambda i: (i, 0)),
                   pl.BlockSpec((NORM_TM, STAT_LANES), lambda i: (i, 0))),
        compiler_params=_params(("parallel",)),
        name="prenorm",
    )(x, g.reshape(1, D_MODEL))


def _rstd(ssq_ref):
    return lax.rsqrt(ssq_ref[...] * (1.0 / D_MODEL) + EPS)


def _dot_chunks(h_ref, w_ref, o_ref, epilogue):
    for c in range(o_ref.shape[1] // PROJ_CHUNK):
        cols = slice(c * PROJ_CHUNK, (c + 1) * PROJ_CHUNK)
        acc = jnp.dot(h_ref[...], w_ref[:, cols], preferred_element_type=F32)
        o_ref[:, cols] = epilogue(acc, cols).astype(o_ref.dtype)


def _proj_kernel(xg_ref, ssq_ref, w_ref, o_ref):
    rstd = jnp.concatenate([_rstd(ssq_ref)] * (PROJ_CHUNK // STAT_LANES), axis=1)
    _dot_chunks(xg_ref, w_ref, o_ref, lambda acc, cols: acc * rstd)


def _proj_lr_kernel(xg_ref, ssq_ref, w_ref, wlr_ref, o_ref, olr_ref):
    _proj_kernel(xg_ref, ssq_ref, w_ref, o_ref)

    @pl.when(pl.program_id(1) == 0)
    def _():
        lr = jnp.dot(xg_ref[...], wlr_ref[...], preferred_element_type=F32) * _rstd(ssq_ref)
        olr_ref[...] = lr.astype(olr_ref.dtype)


def _proj_gate_kernel(xg_ref, ssq_ref, w_ref, b_ref, o_ref):
    rstd = jnp.concatenate([_rstd(ssq_ref)] * (PROJ_CHUNK // STAT_LANES), axis=1)
    _dot_chunks(xg_ref, w_ref, o_ref, lambda acc, cols: jax.nn.sigmoid(acc * rstd + b_ref[:, cols]))


def _proj(xg, ssq, w, w_lr=None, b_gate=None):
    m = xg.shape[0]
    n = w.shape[1]
    assert LR_PAD == STAT_LANES
    h_spec = pl.BlockSpec((PROJ_TM, D_MODEL), lambda i, j: (i, 0))
    s_spec = pl.BlockSpec((PROJ_TM, STAT_LANES), lambda i, j: (i, 0))
    w_spec = pl.BlockSpec((D_MODEL, PROJ_TN), lambda i, j: (0, j))
    o_spec = pl.BlockSpec((PROJ_TM, PROJ_TN), lambda i, j: (i, j))
    o_shape = jax.ShapeDtypeStruct((m, n), BF16)
    common = dict(grid=(m // PROJ_TM, n // PROJ_TN), compiler_params=_params(("parallel", "arbitrary")))
    if w_lr is not None:
        return pl.pallas_call(
            _proj_lr_kernel,
            out_shape=(o_shape, jax.ShapeDtypeStruct((m, LR_PAD), BF16)),
            in_specs=[h_spec, s_spec, w_spec, pl.BlockSpec((D_MODEL, LR_PAD), lambda i, j: (0, 0))],
            out_specs=(o_spec, pl.BlockSpec((PROJ_TM, LR_PAD), lambda i, j: (i, 0))),
            name="proj_ab", **common)(xg, ssq, w, w_lr)
    if b_gate is not None:
        return pl.pallas_call(
            _proj_gate_kernel, out_shape=o_shape,
            in_specs=[h_spec, s_spec, w_spec, pl.BlockSpec((1, PROJ_TN), lambda i, j: (0, j))],
            out_specs=o_spec, name="proj_gate", **common)(xg, ssq, w, b_gate)
    return pl.pallas_call(
        _proj_kernel, out_shape=o_shape, in_specs=[h_spec, s_spec, w_spec], out_specs=o_spec,
        name="proj_c", **common)(xg, ssq, w)


def _sgu_kernel(u_ref, v_ref, z_ref, lng_ref, ws_ref, bs_ref, o_ref, vn_ref):
    v = v_ref[...].astype(F32)
    mu = jnp.mean(v, axis=-1, keepdims=True)
    xc = v - mu
    var = jnp.mean(xc * xc, axis=-1, keepdims=True)
    vn_ref[...] = (xc * lax.rsqrt(var + EPS) * lng_ref[...]).astype(vn_ref.dtype)
    for c in range(SGU_TM // SGU_CHUNK):
        rows = slice(c * SGU_CHUNK, (c + 1) * SGU_CHUNK)
        for g in range(SGU_GROUPS):
            cols = slice(g * SGU_GROUP_W, (g + 1) * SGU_GROUP_W)
            s = jnp.dot(ws_ref[g], vn_ref[rows, cols], preferred_element_type=F32) + bs_ref[:, cols]
            u = u_ref[rows, cols].astype(F32)
            z = z_ref[rows, cols].astype(F32)
            o_ref[rows, cols] = (u * s * _silu(z)).astype(o_ref.dtype)


def _sgu(proj, ln_g, ws, bs_full):
    m = proj.shape[0]
    return pl.pallas_call(
        _sgu_kernel,
        out_shape=jax.ShapeDtypeStruct((m, BRANCH_W), BF16),
        grid=(m // SGU_TM,),
        in_specs=[pl.BlockSpec((SGU_TM, BRANCH_W), lambda i: (i, COL_A_U // BRANCH_W)),
                  pl.BlockSpec((SGU_TM, BRANCH_W), lambda i: (i, COL_A_V // BRANCH_W)),
                  pl.BlockSpec((SGU_TM, BRANCH_W), lambda i: (i, COL_A_Z // BRANCH_W)),
                  pl.BlockSpec((1, BRANCH_W), lambda i: (0, 0)),
                  pl.BlockSpec((SGU_GROUPS, SGU_CHUNK, SGU_CHUNK), lambda i: (0, 0, 0)),
                  pl.BlockSpec((SGU_CHUNK, BRANCH_W), lambda i: (0, 0))],
        out_specs=pl.BlockSpec((SGU_TM, BRANCH_W), lambda i: (i, 0)),
        scratch_shapes=[pltpu.VMEM((SGU_TM, BRANCH_W), BF16)],
        compiler_params=_params(("parallel",)),
        name="sgu",
    )(proj, proj, proj, ln_g, ws, bs_full)


_NT = (((1,), (1,)), ((), ()))
_TN = (((0,), (0,)), ((), ()))


def _split3_dot(mat, x):
    x1 = x.astype(BF16)
    r1 = x - x1.astype(F32)
    x2 = r1.astype(BF16)
    x3 = (r1 - x2.astype(F32)).astype(BF16)
    return (jnp.dot(mat, x1, preferred_element_type=F32)
            + jnp.dot(mat, x2, preferred_element_type=F32)
            + jnp.dot(mat, x3, preferred_element_type=F32))


def _gla_masks():
    ri = jnp.arange(GLA_TB)[:, None]
    ci = jnp.arange(GLA_TB)[None, :]
    same = (ri // GLA_CHUNK) == (ci // GLA_CHUNK)
    return jnp.stack([same & (ci <= ri), same & (ci >= ri)]).astype(F32)


def _gla_kernel(qf_ref, kf_ref, vf_ref, lrf_ref, qb_ref, kb_ref, vb_ref, lrb_ref, z_ref, wgk_ref,
                bgk_ref, ng_ref, tri_ref, trib_ref, o_ref, st_ref, part_ref, *, nb):
    j = pl.program_id(2)
    tb = GLA_TB
    nchunk = tb // GLA_CHUNK
    dirs = (0, 1)

    @pl.when(j == 0)
    def _():
        st_ref[...] = jnp.zeros_like(st_ref)

    q_refs, k_refs, v_refs, lr_refs = (qf_ref, qb_ref), (kf_ref, kb_ref), (vf_ref, vb_ref), (lrf_ref, lrb_ref)
    rows = [slice(c * GLA_CHUNK, (c + 1) * GLA_CHUNK) for c in range(nchunk)]
    orders = (list(range(nchunk)), list(range(nchunk - 1, -1, -1)))
    last = (GLA_CHUNK - 1, 0)

    s = [dict(pieces=[None] * nchunk), dict(pieces=[None] * nchunk)]

    def gate(d):
        x = jnp.dot(lr_refs[d][...], wgk_ref[d], preferred_element_type=F32) + bgk_ref[d]
        s[d]["g"] = (jnp.minimum(x, 0.0) - jnp.log(1.0 + jnp.exp(-jnp.abs(x)))) * (1.0 / GLA_GATE_TEMP)

    def cumsum(d):
        s[d]["gc"] = _split3_dot(trib_ref[d], s[d]["g"])

    def decays(d):
        gc = s[d]["gc"]
        gl = [gc[c * GLA_CHUNK + last[d]:c * GLA_CHUNK + last[d] + 1, :] for c in range(nchunk)]
        gt = jnp.concatenate([jnp.broadcast_to(t, (GLA_CHUNK, GLA_HEAD_K)) for t in gl], axis=0)
        q = q_refs[d][...].astype(F32) * (GLA_HEAD_K ** -0.5)
        k = k_refs[d][...].astype(F32)
        s[d].update(gl=gl, v=v_refs[d][...],
                    q_e=(q * jnp.exp(gc)).astype(BF16),
                    k_e=(k * jnp.exp(-gc)).astype(BF16),
                    k_d=(k * jnp.exp(gt - gc)).astype(BF16))

    def scores(d):
        a = lax.dot_general(s[d]["q_e"], s[d]["k_e"], _NT, preferred_element_type=F32)
        s[d]["att"] = jnp.where(tri_ref[d] > 0.5, a, 0.0).astype(BF16)

    def intra(d):
        s[d]["o_intra"] = jnp.dot(s[d]["att"], s[d]["v"], preferred_element_type=F32)

    def update(d, idx):
        c = orders[d][idx]
        s[d]["u"] = lax.dot_general(s[d]["v"][rows[c]], s[d]["k_d"][rows[c]], _TN,
                                    preferred_element_type=F32)

    def chunk(d, idx):
        c = orders[d][idx]
        u_cur = s[d]["u"]
        if idx + 1 < nchunk:
            update(d, idx + 1)
        st = st_ref[d]
        s[d]["pieces"][c] = s[d]["o_intra"][rows[c]] + lax.dot_general(
            s[d]["q_e"][rows[c]], st.astype(BF16), _NT, preferred_element_type=F32)
        st_ref[d] = st * jnp.exp(s[d]["gl"][c]) + u_cur

    stages = [gate, cumsum, decays, scores, intra, lambda d: update(d, 0)]
    stages += [functools.partial(lambda idx, d: chunk(d, idx), idx) for idx in range(nchunk)]
    for t in range(len(stages) + GLA_SKEW):
        if t < len(stages):
            stages[t](0)
        if 0 <= t - GLA_SKEW < len(stages):
            stages[t - GLA_SKEW](1)
    outs = [jnp.concatenate(s[d]["pieces"], axis=0) for d in dirs]
    toks = (pl.multiple_of(j * tb, tb), pl.multiple_of((nb - 1 - j) * tb, tb))

    @pl.when(j < nb // 2)
    def _():
        for d in dirs:
            part_ref[pl.ds(toks[d], tb), :] = outs[d]

    @pl.when(j >= nb // 2)
    def _():
        for d in dirs:
            tot = part_ref[pl.ds(toks[d], tb), :] + outs[d]
            ms = jnp.mean(tot * tot, axis=-1, keepdims=True)
            y = tot * lax.rsqrt(ms + EPS) * ng_ref[...]
            z = z_ref[pl.ds(toks[d], tb), :].astype(F32)
            o_ref[pl.ds(toks[d], tb), :] = (y * _silu(z)).astype(o_ref.dtype)


def _gla(proj, lr, wgk_pad, bgk, norm_g, batch, seq):
    m = proj.shape[0]
    nb = seq // GLA_TB
    assert nb % 2 == 0
    kq, kv = GLA_HEAD_K, GLA_HEAD_V
    masks = _gla_masks()

    def fwd(b, j):
        return b * nb + j

    def bwd(b, j):
        return b * nb + nb - 1 - j

    def blocks(tok):
        return [pl.BlockSpec((GLA_TB, kq), lambda b, h, j: (tok(b, j), COL_B_Q // kq + h)),
                pl.BlockSpec((GLA_TB, kq), lambda b, h, j: (tok(b, j), COL_B_K // kq + h)),
                pl.BlockSpec((GLA_TB, kv), lambda b, h, j: (tok(b, j), COL_B_V // kv + h)),
                pl.BlockSpec((GLA_TB, LR_PAD), lambda b, h, j: (tok(b, j), 0))]

    return pl.pallas_call(
        functools.partial(_gla_kernel, nb=nb),
        out_shape=jax.ShapeDtypeStruct((m, BRANCH_W), BF16),
        grid=(batch, GLA_HEADS, nb),
        in_specs=blocks(fwd) + blocks(bwd) + [
            pl.BlockSpec((seq, kv), lambda b, h, j: (b, COL_B_Z // kv + h)),
            pl.BlockSpec((2, LR_PAD, kq), lambda b, h, j: (0, 0, h)),
            pl.BlockSpec((2, 1, kq), lambda b, h, j: (0, 0, h)),
            pl.BlockSpec((1, kv), lambda b, h, j: (0, 0)),
            pl.BlockSpec((2, GLA_TB, GLA_TB), lambda b, h, j: (0, 0, 0)),
            pl.BlockSpec((2, GLA_TB, GLA_TB), lambda b, h, j: (0, 0, 0))],
        out_specs=pl.BlockSpec((seq, kv), lambda b, h, j: (b, h)),
        scratch_shapes=[pltpu.VMEM((2, kv, kq), F32), pltpu.VMEM((seq, kv), F32)],
        compiler_params=_params(("parallel", "parallel", "arbitrary")),
        name="gla",
    )(proj, proj, proj, lr, proj, proj, proj, lr, proj, wgk_pad, bgk, norm_g, masks, masks.astype(BF16))


def _na_kernel(q_ref, k_ref, v_ref, z_ref, bias_ref, o_ref, *, rows):
    win = NA_ROWS * GRID_W
    scale = NA_HEAD_DIM ** -0.5

    def group(g, carry):
        q0s, k0s, scores = [], [], []
        for i in range(NA_GROUP):
            r = g * NA_GROUP + i
            rs = jnp.clip(r - NA_ROWS // 2, 0, rows - NA_ROWS)
            q0 = pl.multiple_of(r * GRID_W, GRID_W)
            k0 = pl.multiple_of(rs * GRID_W, GRID_W)
            s = lax.dot_general(q_ref[pl.ds(q0, GRID_W), :], k_ref[pl.ds(k0, win), :], _NT,
                                preferred_element_type=F32)
            scores.append(s * (scale * LOG2E) + bias_ref[0, r - rs])
            q0s.append(q0)
            k0s.append(k0)
        probs, inv = [], []
        for s in scores:
            p = jnp.exp2(s - jnp.max(s, axis=-1, keepdims=True))
            inv.append(1.0 / jnp.sum(p, axis=-1, keepdims=True))
            probs.append(p.astype(BF16))
        for i in range(NA_GROUP):
            o = jnp.dot(probs[i], v_ref[pl.ds(k0s[i], win), :], preferred_element_type=F32) * inv[i]
            z = z_ref[pl.ds(q0s[i], GRID_W), :].astype(F32)
            o_ref[pl.ds(q0s[i], GRID_W), :] = (o * _silu(z)).astype(o_ref.dtype)
        return carry

    lax.fori_loop(0, rows // NA_GROUP, group, 0)


def _na(proj, bias_tab, batch, seq):
    m = proj.shape[0]
    rows = seq // GRID_W
    assert rows % NA_GROUP == 0 and rows >= NA_ROWS
    hd = NA_HEAD_DIM
    return pl.pallas_call(
        functools.partial(_na_kernel, rows=rows),
        out_shape=jax.ShapeDtypeStruct((m, BRANCH_W), BF16),
        grid=(NA_HEADS, batch),
        in_specs=[pl.BlockSpec((seq, hd), lambda h, b: (b, COL_C_Q // hd + h)),
                  pl.BlockSpec((seq, hd), lambda h, b: (b, COL_C_K // hd + h)),
                  pl.BlockSpec((seq, hd), lambda h, b: (b, COL_C_V // hd + h)),
                  pl.BlockSpec((seq, hd), lambda h, b: (b, COL_C_Z // hd + h)),
                  pl.BlockSpec((1, NA_ROWS, GRID_W, NA_ROWS * GRID_W), lambda h, b: (h, 0, 0, 0))],
        out_specs=pl.BlockSpec((seq, hd), lambda h, b: (b, h)),
        compiler_params=_params(("parallel", "parallel")),
        name="na",
    )(proj, proj, proj, proj, bias_tab)


def _merge_kernel(ya_ref, yb_ref, yc_ref, wb_ref, ga_ref, gb_ref, gc_ref, o_ref):
    y_refs = (ya_ref, yb_ref, yc_ref)
    g_refs = (ga_ref, gb_ref, gc_ref)
    for c in range(o_ref.shape[1] // PROJ_CHUNK):
        cols = slice(c * PROJ_CHUNK, (c + 1) * PROJ_CHUNK)
        acc = None
        for b in range(N_BRANCH):
            t = g_refs[b][:, cols].astype(F32) * jnp.dot(y_refs[b][...], wb_ref[b, :, cols],
                                                         preferred_element_type=F32)
            acc = t if acc is None else acc + t
        o_ref[:, cols] = acc.astype(o_ref.dtype)


def _merge(ya, yb, yc, wb, gates):
    m = ya.shape[0]
    tm, tn = MERGE_TM, MERGE_TN
    y_spec = pl.BlockSpec((tm, BRANCH_W), lambda i, j: (i, 0))

    def gate_spec(b):
        return pl.BlockSpec((tm, tn), lambda i, j: (i, b * D_MODEL // tn + j))

    return pl.pallas_call(
        _merge_kernel,
        out_shape=jax.ShapeDtypeStruct((m, D_MODEL), BF16),
        grid=(m // tm, D_MODEL // tn),
        in_specs=[y_spec, y_spec, y_spec,
                  pl.BlockSpec((N_BRANCH, BRANCH_W, tn), lambda i, j: (0, 0, j)),
                  gate_spec(0), gate_spec(1), gate_spec(2)],
        out_specs=pl.BlockSpec((tm, tn), lambda i, j: (i, j)),
        compiler_params=_params(("parallel", "arbitrary")),
        name="merge",
    )(ya, yb, yc, wb, gates, gates, gates)


def _out_kernel(m_ref, w_ref, x_ref, o_ref):
    _dot_chunks(m_ref, w_ref, o_ref, lambda acc, cols: x_ref[:, cols] + acc)


def _out_prenorm_kernel(m_ref, w_ref, x_ref, g_ref, o_ref, xg_ref, ssq_ref):
    @pl.when(pl.program_id(1) == 0)
    def _():
        ssq_ref[...] = jnp.zeros_like(ssq_ref)

    ssq = None
    for c in range(o_ref.shape[1] // PROJ_CHUNK):
        cols = slice(c * PROJ_CHUNK, (c + 1) * PROJ_CHUNK)
        xn = x_ref[:, cols] + jnp.dot(m_ref[...], w_ref[:, cols], preferred_element_type=F32)
        o_ref[:, cols] = xn
        xg_ref[:, cols] = (xn * g_ref[:, cols]).astype(xg_ref.dtype)
        s = jnp.sum(xn * xn, axis=-1, keepdims=True)
        ssq = s if ssq is None else ssq + s
    ssq_ref[...] += jnp.broadcast_to(ssq, ssq_ref.shape)


def _out(merged, wo, x, g_next=None):
    m = x.shape[0]
    tm, tn = OUT_TM, OUT_TN
    in_specs = [pl.BlockSpec((tm, D_MODEL), lambda i, j: (i, 0)),
                pl.BlockSpec((D_MODEL, tn), lambda i, j: (0, j)),
                pl.BlockSpec((tm, tn), lambda i, j: (i, j))]
    x_spec = pl.BlockSpec((tm, tn), lambda i, j: (i, j))
    x_shape = jax.ShapeDtypeStruct((m, D_MODEL), F32)
    common = dict(grid=(m // tm, D_MODEL // tn), compiler_params=_params(("parallel", "arbitrary")))
    if g_next is None:
        return pl.pallas_call(_out_kernel, out_shape=x_shape, in_specs=in_specs, out_specs=x_spec,
                              name="outproj", **common)(merged, wo, x)
    return pl.pallas_call(
        _out_prenorm_kernel,
        out_shape=(x_shape, jax.ShapeDtypeStruct((m, D_MODEL), BF16),
                   jax.ShapeDtypeStruct((m, STAT_LANES), F32)),
        in_specs=in_specs + [pl.BlockSpec((1, tn), lambda i, j: (0, j))],
        out_specs=(x_spec, pl.BlockSpec((tm, tn), lambda i, j: (i, j)),
                   pl.BlockSpec((tm, STAT_LANES), lambda i, j: (i, 0))),
        name="outproj_prenorm", **common)(merged, wo, x, g_next.reshape(1, D_MODEL))


def _cast_shift_kernel(a_ref, b_ref, o_ref, *, shift):
    a = a_ref[...]
    b = b_ref[...]
    o_ref[...] = jnp.concatenate([a[:, shift:], b[:, :shift]], axis=1).astype(o_ref.dtype)


def _cast_cols(w, col0, ncols):
    rows = w.shape[0]
    tr, tc = CAST_TR, CAST_TC
    shift = col0 % V7X_LANES
    base = col0 - shift
    assert shift and base % tc == 0 and ncols % tc == 0 and rows % tr == 0
    return pl.pallas_call(
        functools.partial(_cast_shift_kernel, shift=shift),
        out_shape=jax.ShapeDtypeStruct((rows, ncols), BF16),
        grid=(rows // tr, ncols // tc),
        in_specs=[pl.BlockSpec((tr, tc), lambda i, j: (i, base // tc + j)),
                  pl.BlockSpec((tr, V7X_LANES), lambda i, j: (i, (base + tc * (j + 1)) // V7X_LANES))],
        out_specs=pl.BlockSpec((tr, tc), lambda i, j: (i, j)),
        compiler_params=_params(("parallel", "parallel")),
        name="cast_cols",
    )(w, w)


def _na_bias_table(rpb):
    cols = jnp.arange(GRID_W)
    cs = jnp.clip(cols - NA_COLS // 2, 0, GRID_W - NA_COLS)
    col_ok = (cols[None, :] >= cs[:, None]) & (cols[None, :] < cs[:, None] + NA_COLS)
    dc = jnp.clip(cols[None, :] - cols[:, None] + NA_COLS - 1, 0, 2 * NA_COLS - 2)
    toe = jnp.zeros(rpb.shape[:3] + (GRID_W, GRID_W), F32)
    for c in range(2 * NA_COLS - 1):
        toe = jnp.where(dc == c, rpb[:, :, :, c, None, None], toe)
    toe = jnp.where(col_ok, toe * LOG2E, NEG_BIAS)
    return jnp.stack(
        [jnp.concatenate([toe[:, :, n - d + NA_ROWS - 1] for n in range(NA_ROWS)], axis=-1)
         for d in range(NA_ROWS)], axis=2)


def _trunk(x, batch, seq, p):
    xg, ssq = _prenorm(x, p["norm_g"][0])
    for l in range(DEPTH):
        pab, lr = _proj(xg, ssq, p["w_ab"][l], w_lr=p["w_lr"][l])
        pc = _proj(xg, ssq, p["w_c"][l])
        gates = _proj(xg, ssq, p["w_gate"][l], b_gate=p["b_gate"][l])
        ya = _sgu(pab, p["sgu_ln_g"][l], p["sgu_w"][l], p["sgu_b"][l])
        yb = _gla(pab, lr, p["wgk"][l], p["bgk"][l], p["gla_norm_g"][l], batch, seq)
        yc = _na(pc, p["na_bias"][l], batch, seq)
        merged = _merge(ya, yb, yc, p["w_branch"][l], gates)
        if l + 1 < DEPTH:
            x, xg, ssq = _out(merged, p["w_out"][l], x, g_next=p["norm_g"][l + 1])
        else:
            x = _out(merged, p["w_out"][l], x)
    return _rmsnorm(x, p["final_norm_g"], F32)


def kernel(x_prompt, x_sample, norm_g, w_in, sgu_ln_g, sgu_w, sgu_b, gla_w_gk, gla_b_gk, gla_norm_g,
           na_rpb, w_gate, b_gate, w_branch, w_out, final_norm_g):
    lr0 = AB_COLS
    lr1 = lr0 + 2 * GLA_LOW_RANK
    w_lr = jnp.pad(w_in[:, :, lr0:lr1], ((0, 0), (0, 0), (0, LR_PAD - 2 * GLA_LOW_RANK))).astype(BF16)
    wgk = jnp.zeros((DEPTH, 2, LR_PAD, GLA_KEY_W), F32)
    wgk = wgk.at[:, 0, :GLA_LOW_RANK].set(gla_w_gk[:, 0])
    wgk = wgk.at[:, 1, GLA_LOW_RANK:2 * GLA_LOW_RANK].set(gla_w_gk[:, 1])
    p = {
        "norm_g": norm_g,
        "final_norm_g": final_norm_g,
        "w_ab": w_in[:, :, :lr0].astype(BF16),
        "w_c": _cast_cols(w_in.reshape(DEPTH * D_MODEL, -1), lr1, C_COLS).reshape(DEPTH, D_MODEL, C_COLS),
        "w_gate": w_gate.astype(BF16),
        "w_lr": w_lr,
        "b_gate": b_gate.reshape(DEPTH, 1, N_BRANCH * D_MODEL),
        "sgu_ln_g": sgu_ln_g.reshape(DEPTH, 1, BRANCH_W),
        "sgu_w": sgu_w.astype(BF16),
        "sgu_b": jnp.repeat(jnp.swapaxes(sgu_b, 1, 2), SGU_GROUP_W, axis=2),
        "wgk": wgk.astype(BF16),
        "bgk": gla_b_gk.reshape(DEPTH, 2, 1, GLA_KEY_W),
        "gla_norm_g": gla_norm_g.reshape(DEPTH, 1, GLA_HEAD_V),
        "na_bias": _na_bias_table(na_rpb),
        "w_branch": w_branch.astype(BF16),
        "w_out": w_out.astype(BF16),
    }
    bp, tp, _ = x_prompt.shape
    bs, ts, _ = x_sample.shape
    y_prompt = _trunk(x_prompt.reshape(bp * tp, D_MODEL), bp, tp, p).reshape(x_prompt.shape)
    y_sample = _trunk(x_sample.reshape(bs * ts, D_MODEL), bs, ts, p).reshape(x_sample.shape)
    return (y_prompt, y_sample)
```

```python
import functools

import jax
import jax.numpy as jnp
from jax import lax
from jax.experimental import pallas as pl
from jax.experimental.pallas import tpu as pltpu

F32 = jnp.float32
BF16 = jnp.bfloat16

D_MODEL = 4096
DEPTH = 4
BRANCH_W = D_MODEL // 2
N_BRANCH = 3
GRID_W = 64
SGU_CHUNK = 128
SGU_GROUPS = 8
SGU_GROUP_W = BRANCH_W // SGU_GROUPS
GLA_HEADS = 4
GLA_KEY_W = BRANCH_W // 2
GLA_HEAD_K = GLA_KEY_W // GLA_HEADS
GLA_HEAD_V = BRANCH_W // GLA_HEADS
GLA_LOW_RANK = 16
GLA_GATE_TEMP = 16.0
GLA_CHUNK = 64
NA_HEAD_DIM = 128
NA_HEADS = BRANCH_W // NA_HEAD_DIM
NA_ROWS = 8
NA_COLS = 16
EPS = 1e-6
LOG2E = 1.4426950408889634

COL_A_U = 0
COL_A_V = COL_A_U + BRANCH_W
COL_A_Z = COL_A_V + BRANCH_W
COL_B_Q = COL_A_Z + BRANCH_W
COL_B_K = COL_B_Q + GLA_KEY_W
COL_B_V = COL_B_K + GLA_KEY_W
COL_B_Z = COL_B_V + BRANCH_W
AB_COLS = COL_B_Z + BRANCH_W
COL_C_Q = 0
COL_C_K = COL_C_Q + BRANCH_W
COL_C_V = COL_C_K + BRANCH_W
COL_C_Z = COL_C_V + BRANCH_W
C_COLS = COL_C_Z + BRANCH_W
LR_PAD = 128
STAT_LANES = 128

VMEM_LIMIT = 56 * 1024 * 1024
NEG_BIAS = -1e30

PROJ_TM, PROJ_TN = 1024, 1024
PROJ_CHUNK = 256
NORM_TM = 256
SGU_TM = 512
GLA_TB = 256
GLA_SKEW = 0
NA_GROUP = 32
MERGE_TM, MERGE_TN = 1024, 512
OUT_TM, OUT_TN = 1024, 1024


def _params(sem):
    return pltpu.CompilerParams(dimension_semantics=sem, vmem_limit_bytes=VMEM_LIMIT)


def _silu(z):
    return z * jax.nn.sigmoid(z)


def _rmsnorm_kernel(x_ref, g_ref, o_ref):
    x = x_ref[...]
    ms = jnp.mean(x * x, axis=-1, keepdims=True)
    o_ref[...] = (x * lax.rsqrt(ms + EPS) * g_ref[...]).astype(o_ref.dtype)


def _rmsnorm(x, g, out_dtype):
    m = x.shape[0]
    return pl.pallas_call(
        _rmsnorm_kernel,
        out_shape=jax.ShapeDtypeStruct((m, D_MODEL), out_dtype),
        grid=(m // NORM_TM,),
        in_specs=[pl.BlockSpec((NORM_TM, D_MODEL), lambda i: (i, 0)),
                  pl.BlockSpec((1, D_MODEL), lambda i: (0, 0))],
        out_specs=pl.BlockSpec((NORM_TM, D_MODEL), lambda i: (i, 0)),
        compiler_params=_params(("parallel",)),
        name="rmsnorm",
    )(x, g.reshape(1, D_MODEL))


def _prenorm_kernel(x_ref, g_ref, xg_ref, ssq_ref):
    x = x_ref[...]
    xg_ref[...] = (x * g_ref[...]).astype(xg_ref.dtype)
    ssq_ref[...] = jnp.broadcast_to(jnp.sum(x * x, axis=-1, keepdims=True), ssq_ref.shape)


def _prenorm(x, g):
    m = x.shape[0]
    return pl.pallas_call(
        _prenorm_kernel,
        out_shape=(jax.ShapeDtypeStruct((m, D_MODEL), BF16), jax.ShapeDtypeStruct((m, STAT_LANES), F32)),
        grid=(m // NORM_TM,),
        in_specs=[pl.BlockSpec((NORM_TM, D_MODEL), lambda i: (i, 0)),
                  pl.BlockSpec((1, D_MODEL), lambda i: (0, 0))],
        out_specs=(pl.BlockSpec((NORM_TM, D_MODEL), lambda i: (i, 0)),
                   pl.BlockSpec((NORM_TM, STAT_LANES), lambda i: (i, 0))),
        compiler_params=_params(("parallel",)),
        name="prenorm",
    )(x, g.reshape(1, D_MODEL))


def _rstd(ssq_ref):
    return lax.rsqrt(ssq_ref[...] * (1.0 / D_MODEL) + EPS)


_NT = (((1,), (1,)), ((), ()))
_TN = (((0,), (0,)), ((), ()))


def _dot_chunks(h_ref, w_ref, o_ref, epilogue, transposed=False):
    for c in range(o_ref.shape[1] // PROJ_CHUNK):
        cols = slice(c * PROJ_CHUNK, (c + 1) * PROJ_CHUNK)
        if transposed:
            acc = lax.dot_general(h_ref[...], w_ref[cols, :], _NT, preferred_element_type=F32)
        else:
            acc = jnp.dot(h_ref[...], w_ref[:, cols], preferred_element_type=F32)
        o_ref[:, cols] = epilogue(acc, cols).astype(o_ref.dtype)


def _proj_kernel(xg_ref, ssq_ref, wt_ref, o_ref):
    rstd = jnp.concatenate([_rstd(ssq_ref)] * (PROJ_CHUNK // STAT_LANES), axis=1)
    _dot_chunks(xg_ref, wt_ref, o_ref, lambda acc, cols: acc * rstd, transposed=True)


def _proj_lr_kernel(xg_ref, ssq_ref, wt_ref, wlrt_ref, o_ref, olr_ref):
    _proj_kernel(xg_ref, ssq_ref, wt_ref, o_ref)

    @pl.when(pl.program_id(1) == 0)
    def _():
        lr = lax.dot_general(xg_ref[...], wlrt_ref[...], _NT, preferred_element_type=F32)
        olr_ref[...] = (lr * _rstd(ssq_ref)).astype(olr_ref.dtype)


def _proj_gate_kernel(xg_ref, ssq_ref, w_ref, b_ref, o_ref):
    rstd = jnp.concatenate([_rstd(ssq_ref)] * (PROJ_CHUNK // STAT_LANES), axis=1)
    _dot_chunks(xg_ref, w_ref, o_ref, lambda acc, cols: jax.nn.sigmoid(acc * rstd + b_ref[:, cols]))


def _proj_specs(m, n):
    assert LR_PAD == STAT_LANES
    h_spec = pl.BlockSpec((PROJ_TM, D_MODEL), lambda i, j: (i, 0))
    s_spec = pl.BlockSpec((PROJ_TM, STAT_LANES), lambda i, j: (i, 0))
    o_spec = pl.BlockSpec((PROJ_TM, PROJ_TN), lambda i, j: (i, j))
    common = dict(grid=(m // PROJ_TM, n // PROJ_TN), compiler_params=_params(("parallel", "arbitrary")))
    return h_spec, s_spec, o_spec, jax.ShapeDtypeStruct((m, n), BF16), common


def _proj_in(xg, ssq, wt, l, n, lr_row=None):
    m = xg.shape[0]
    h_spec, s_spec, o_spec, o_shape, common = _proj_specs(m, n)
    wt_spec = pl.BlockSpec((None, PROJ_TN, D_MODEL), lambda i, j: (l, j, 0))
    if lr_row is None:
        return pl.pallas_call(_proj_kernel, out_shape=o_shape, in_specs=[h_spec, s_spec, wt_spec],
                              out_specs=o_spec, name="proj_c", **common)(xg, ssq, wt)
    return pl.pallas_call(
        _proj_lr_kernel,
        out_shape=(o_shape, jax.ShapeDtypeStruct((m, LR_PAD), BF16)),
        in_specs=[h_spec, s_spec, wt_spec,
                  pl.BlockSpec((None, LR_PAD, D_MODEL), lambda i, j: (l, lr_row // LR_PAD, 0))],
        out_specs=(o_spec, pl.BlockSpec((PROJ_TM, LR_PAD), lambda i, j: (i, 0))),
        name="proj_ab", **common)(xg, ssq, wt, wt)


def _proj_gate(xg, ssq, w, b, l):
    m = xg.shape[0]
    n = w.shape[2]
    h_spec, s_spec, o_spec, o_shape, common = _proj_specs(m, n)
    return pl.pallas_call(
        _proj_gate_kernel, out_shape=o_shape,
        in_specs=[h_spec, s_spec, pl.BlockSpec((None, D_MODEL, PROJ_TN), lambda i, j: (l, 0, j)),
                  pl.BlockSpec((None, 1, PROJ_TN), lambda i, j: (l, 0, j))],
        out_specs=o_spec, name="proj_gate", **common)(xg, ssq, w, b)


def _sgu_kernel(u_ref, v_ref, z_ref, lng_ref, ws_ref, bs_ref, o_ref, vn_ref):
    v = v_ref[...].astype(F32)
    mu = jnp.mean(v, axis=-1, keepdims=True)
    xc = v - mu
    var = jnp.mean(xc * xc, axis=-1, keepdims=True)
    vn_ref[...] = (xc * lax.rsqrt(var + EPS) * lng_ref[...]).astype(vn_ref.dtype)
    for c in range(SGU_TM // SGU_CHUNK):
        rows = slice(c * SGU_CHUNK, (c + 1) * SGU_CHUNK)
        for g in range(SGU_GROUPS):
            cols = slice(g * SGU_GROUP_W, (g + 1) * SGU_GROUP_W)
            s = jnp.dot(ws_ref[g], vn_ref[rows, cols], preferred_element_type=F32) + bs_ref[:, cols]
            u = u_ref[rows, cols].astype(F32)
            z = z_ref[rows, cols].astype(F32)
            o_ref[rows, cols] = (u * s * _silu(z)).astype(o_ref.dtype)


def _sgu(proj, ln_g, ws, bs_full):
    m = proj.shape[0]
    return pl.pallas_call(
        _sgu_kernel,
        out_shape=jax.ShapeDtypeStruct((m, BRANCH_W), BF16),
        grid=(m // SGU_TM,),
        in_specs=[pl.BlockSpec((SGU_TM, BRANCH_W), lambda i: (i, COL_A_U // BRANCH_W)),
                  pl.BlockSpec((SGU_TM, BRANCH_W), lambda i: (i, COL_A_V // BRANCH_W)),
                  pl.BlockSpec((SGU_TM, BRANCH_W), lambda i: (i, COL_A_Z // BRANCH_W)),
                  pl.BlockSpec((1, BRANCH_W), lambda i: (0, 0)),
                  pl.BlockSpec((SGU_GROUPS, SGU_CHUNK, SGU_CHUNK), lambda i: (0, 0, 0)),
                  pl.BlockSpec((SGU_CHUNK, BRANCH_W), lambda i: (0, 0))],
        out_specs=pl.BlockSpec((SGU_TM, BRANCH_W), lambda i: (i, 0)),
        scratch_shapes=[pltpu.VMEM((SGU_TM, BRANCH_W), BF16)],
        compiler_params=_params(("parallel",)),
        name="sgu",
    )(proj, proj, proj, ln_g, ws, bs_full)


def _split3_dot(mat, x):
    x1 = x.astype(BF16)
    r1 = x - x1.astype(F32)
    x2 = r1.astype(BF16)
    x3 = (r1 - x2.astype(F32)).astype(BF16)
    return (jnp.dot(mat, x1, preferred_element_type=F32)
            + jnp.dot(mat, x2, preferred_element_type=F32)
            + jnp.dot(mat, x3, preferred_element_type=F32))


def _gla_masks():
    ri = jnp.arange(GLA_TB)[:, None]
    ci = jnp.arange(GLA_TB)[None, :]
    same = (ri // GLA_CHUNK) == (ci // GLA_CHUNK)
    return jnp.stack([same & (ci <= ri), same & (ci >= ri)]).astype(F32)


def _gla_kernel(qf_ref, kf_ref, vf_ref, lrf_ref, qb_ref, kb_ref, vb_ref, lrb_ref, z_ref, wgk_ref,
                bgk_ref, ng_ref, tri_ref, trib_ref, o_ref, st_ref, part_ref, *, nb):
    j = pl.program_id(2)
    tb = GLA_TB
    nchunk = tb // GLA_CHUNK
    dirs = (0, 1)

    @pl.when(j == 0)
    def _():
        st_ref[...] = jnp.zeros_like(st_ref)

    q_refs, k_refs, v_refs, lr_refs = (qf_ref, qb_ref), (kf_ref, kb_ref), (vf_ref, vb_ref), (lrf_ref, lrb_ref)
    rows = [slice(c * GLA_CHUNK, (c + 1) * GLA_CHUNK) for c in range(nchunk)]
    orders = (list(range(nchunk)), list(range(nchunk - 1, -1, -1)))
    last = (GLA_CHUNK - 1, 0)

    s = [dict(pieces=[None] * nchunk), dict(pieces=[None] * nchunk)]

    def gate(d):
        x = jnp.dot(lr_refs[d][...], wgk_ref[d], preferred_element_type=F32) + bgk_ref[d]
        s[d]["g"] = (jnp.minimum(x, 0.0) - jnp.log(1.0 + jnp.exp(-jnp.abs(x)))) * (1.0 / GLA_GATE_TEMP)

    def cumsum(d):
        s[d]["gc"] = _split3_dot(trib_ref[d], s[d]["g"])

    def decays(d):
        gc = s[d]["gc"]
        gl = [gc[c * GLA_CHUNK + last[d]:c * GLA_CHUNK + last[d] + 1, :] for c in range(nchunk)]
        gt = jnp.concatenate([jnp.broadcast_to(t, (GLA_CHUNK, GLA_HEAD_K)) for t in gl], axis=0)
        q = q_refs[d][...].astype(F32) * (GLA_HEAD_K ** -0.5)
        k = k_refs[d][...].astype(F32)
        s[d].update(gl=gl, v=v_refs[d][...],
                    q_e=(q * jnp.exp(gc)).astype(BF16),
                    k_e=(k * jnp.exp(-gc)).astype(BF16),
                    k_d=(k * jnp.exp(gt - gc)).astype(BF16))

    def scores(d):
        a = lax.dot_general(s[d]["q_e"], s[d]["k_e"], _NT, preferred_element_type=F32)
        s[d]["att"] = jnp.where(tri_ref[d] > 0.5, a, 0.0).astype(BF16)

    def intra(d):
        s[d]["o_intra"] = jnp.dot(s[d]["att"], s[d]["v"], preferred_element_type=F32)

    def update(d, idx):
        c = orders[d][idx]
        s[d]["u"] = lax.dot_general(s[d]["v"][rows[c]], s[d]["k_d"][rows[c]], _TN,
                                    preferred_element_type=F32)

    def chunk(d, idx):
        c = orders[d][idx]
        u_cur = s[d]["u"]
        if idx + 1 < nchunk:
            update(d, idx + 1)
        st = st_ref[d]
        s[d]["pieces"][c] = s[d]["o_intra"][rows[c]] + lax.dot_general(
            s[d]["q_e"][rows[c]], st.astype(BF16), _NT, preferred_element_type=F32)
        st_ref[d] = st * jnp.exp(s[d]["gl"][c]) + u_cur

    stages = [gate, cumsum, decays, scores, intra, lambda d: update(d, 0)]
    stages += [functools.partial(lambda idx, d: chunk(d, idx), idx) for idx in range(nchunk)]
    for t in range(len(stages) + GLA_SKEW):
        if t < len(stages):
            stages[t](0)
        if 0 <= t - GLA_SKEW < len(stages):
            stages[t - GLA_SKEW](1)
    outs = [jnp.concatenate(s[d]["pieces"], axis=0) for d in dirs]
    toks = (pl.multiple_of(j * tb, tb), pl.multiple_of((nb - 1 - j) * tb, tb))

    @pl.when(j < nb // 2)
    def _():
        for d in dirs:
            part_ref[pl.ds(toks[d], tb), :] = outs[d]

    @pl.when(j >= nb // 2)
    def _():
        for d in dirs:
            tot = part_ref[pl.ds(toks[d], tb), :] + outs[d]
            ms = jnp.mean(tot * tot, axis=-1, keepdims=True)
            y = tot * lax.rsqrt(ms + EPS) * ng_ref[...]
            z = z_ref[pl.ds(toks[d], tb), :].astype(F32)
            o_ref[pl.ds(toks[d], tb), :] = (y * _silu(z)).astype(o_ref.dtype)


def _gla(proj, lr, wgk_pad, bgk, norm_g, batch, seq):
    m = proj.shape[0]
    nb = seq // GLA_TB
    assert nb % 2 == 0
    kq, kv = GLA_HEAD_K, GLA_HEAD_V
    masks = _gla_masks()

    def fwd(b, j):
        return b * nb + j

    def bwd(b, j):
        return b * nb + nb - 1 - j

    def blocks(tok):
        return [pl.BlockSpec((GLA_TB, kq), lambda b, h, j: (tok(b, j), COL_B_Q // kq + h)),
                pl.BlockSpec((GLA_TB, kq), lambda b, h, j: (tok(b, j), COL_B_K // kq + h)),
                pl.BlockSpec((GLA_TB, kv), lambda b, h, j: (tok(b, j), COL_B_V // kv + h)),
                pl.BlockSpec((GLA_TB, LR_PAD), lambda b, h, j: (tok(b, j), 0))]

    return pl.pallas_call(
        functools.partial(_gla_kernel, nb=nb),
        out_shape=jax.ShapeDtypeStruct((m, BRANCH_W), BF16),
        grid=(batch, GLA_HEADS, nb),
        in_specs=blocks(fwd) + blocks(bwd) + [
            pl.BlockSpec((seq, kv), lambda b, h, j: (b, COL_B_Z // kv + h)),
            pl.BlockSpec((2, LR_PAD, kq), lambda b, h, j: (0, 0, h)),
            pl.BlockSpec((2, 1, kq), lambda b, h, j: (0, 0, h)),
            pl.BlockSpec((1, kv), lambda b, h, j: (0, 0)),
            pl.BlockSpec((2, GLA_TB, GLA_TB), lambda b, h, j: (0, 0, 0)),
            pl.BlockSpec((2, GLA_TB, GLA_TB), lambda b, h, j: (0, 0, 0))],
        out_specs=pl.BlockSpec((seq, kv), lambda b, h, j: (b, h)),
        scratch_shapes=[pltpu.VMEM((2, kv, kq), F32), pltpu.VMEM((seq, kv), F32)],
        compiler_params=_params(("parallel", "parallel", "arbitrary")),
        name="gla",
    )(proj, proj, proj, lr, proj, proj, proj, lr, proj, wgk_pad, bgk, norm_g, masks, masks.astype(BF16))


def _na_kernel(q_ref, k_ref, v_ref, z_ref, bias_ref, o_ref, *, rows):
    win = NA_ROWS * GRID_W
    scale = NA_HEAD_DIM ** -0.5

    def group(g, carry):
        q0s, k0s, scores = [], [], []
        for i in range(NA_GROUP):
            r = g * NA_GROUP + i
            rs = jnp.clip(r - NA_ROWS // 2, 0, rows - NA_ROWS)
            q0 = pl.multiple_of(r * GRID_W, GRID_W)
            k0 = pl.multiple_of(rs * GRID_W, GRID_W)
            s = lax.dot_general(q_ref[pl.ds(q0, GRID_W), :], k_ref[pl.ds(k0, win), :], _NT,
                                preferred_element_type=F32)
            scores.append(s * (scale * LOG2E) + bias_ref[0, r - rs])
            q0s.append(q0)
            k0s.append(k0)
        probs, inv = [], []
        for s in scores:
            p = jnp.exp2(s - jnp.max(s, axis=-1, keepdims=True))
            inv.append(1.0 / jnp.sum(p, axis=-1, keepdims=True))
            probs.append(p.astype(BF16))
        for i in range(NA_GROUP):
            o = jnp.dot(probs[i], v_ref[pl.ds(k0s[i], win), :], preferred_element_type=F32) * inv[i]
            z = z_ref[pl.ds(q0s[i], GRID_W), :].astype(F32)
            o_ref[pl.ds(q0s[i], GRID_W), :] = (o * _silu(z)).astype(o_ref.dtype)
        return carry

    lax.fori_loop(0, rows // NA_GROUP, group, 0)


def _na(proj, bias_tab, batch, seq):
    m = proj.shape[0]
    rows = seq // GRID_W
    assert rows % NA_GROUP == 0 and rows >= NA_ROWS
    hd = NA_HEAD_DIM
    return pl.pallas_call(
        functools.partial(_na_kernel, rows=rows),
        out_shape=jax.ShapeDtypeStruct((m, BRANCH_W), BF16),
        grid=(NA_HEADS, batch),
        in_specs=[pl.BlockSpec((seq, hd), lambda h, b: (b, COL_C_Q // hd + h)),
                  pl.BlockSpec((seq, hd), lambda h, b: (b, COL_C_K // hd + h)),
                  pl.BlockSpec((seq, hd), lambda h, b: (b, COL_C_V // hd + h)),
                  pl.BlockSpec((seq, hd), lambda h, b: (b, COL_C_Z // hd + h)),
                  pl.BlockSpec((1, NA_ROWS, GRID_W, NA_ROWS * GRID_W), lambda h, b: (h, 0, 0, 0))],
        out_specs=pl.BlockSpec((seq, hd), lambda h, b: (b, h)),
        compiler_params=_params(("parallel", "parallel")),
        name="na",
    )(proj, proj, proj, proj, bias_tab)


def _merge_kernel(ya_ref, yb_ref, yc_ref, wb_ref, ga_ref, gb_ref, gc_ref, o_ref):
    y_refs = (ya_ref, yb_ref, yc_ref)
    g_refs = (ga_ref, gb_ref, gc_ref)
    for c in range(o_ref.shape[1] // PROJ_CHUNK):
        cols = slice(c * PROJ_CHUNK, (c + 1) * PROJ_CHUNK)
        acc = None
        for b in range(N_BRANCH):
            t = g_refs[b][:, cols].astype(F32) * jnp.dot(y_refs[b][...], wb_ref[b, :, cols],
                                                         preferred_element_type=F32)
            acc = t if acc is None else acc + t
        o_ref[:, cols] = acc.astype(o_ref.dtype)


def _merge(ya, yb, yc, wb, l, gates):
    m = ya.shape[0]
    tm, tn = MERGE_TM, MERGE_TN
    y_spec = pl.BlockSpec((tm, BRANCH_W), lambda i, j: (i, 0))

    def gate_spec(b):
        return pl.BlockSpec((tm, tn), lambda i, j: (i, b * D_MODEL // tn + j))

    return pl.pallas_call(
        _merge_kernel,
        out_shape=jax.ShapeDtypeStruct((m, D_MODEL), BF16),
        grid=(m // tm, D_MODEL // tn),
        in_specs=[y_spec, y_spec, y_spec,
                  pl.BlockSpec((None, N_BRANCH, BRANCH_W, tn), lambda i, j: (l, 0, 0, j)),
                  gate_spec(0), gate_spec(1), gate_spec(2)],
        out_specs=pl.BlockSpec((tm, tn), lambda i, j: (i, j)),
        compiler_params=_params(("parallel", "arbitrary")),
        name="merge",
    )(ya, yb, yc, wb, gates, gates, gates)


def _out_kernel(m_ref, w_ref, x_ref, o_ref):
    _dot_chunks(m_ref, w_ref, o_ref, lambda acc, cols: x_ref[:, cols] + acc)


def _out_prenorm_kernel(m_ref, w_ref, x_ref, g_ref, o_ref, xg_ref, ssq_ref):
    @pl.when(pl.program_id(1) == 0)
    def _():
        ssq_ref[...] = jnp.zeros_like(ssq_ref)

    ssq = None
    for c in range(o_ref.shape[1] // PROJ_CHUNK):
        cols = slice(c * PROJ_CHUNK, (c + 1) * PROJ_CHUNK)
        xn = x_ref[:, cols] + jnp.dot(m_ref[...], w_ref[:, cols], preferred_element_type=F32)
        o_ref[:, cols] = xn
        xg_ref[:, cols] = (xn * g_ref[:, cols]).astype(xg_ref.dtype)
        s = jnp.sum(xn * xn, axis=-1, keepdims=True)
        ssq = s if ssq is None else ssq + s
    ssq_ref[...] += jnp.broadcast_to(ssq, ssq_ref.shape)


def _out(merged, wo, l, x, g_next=None):
    m = x.shape[0]
    tm, tn = OUT_TM, OUT_TN
    in_specs = [pl.BlockSpec((tm, D_MODEL), lambda i, j: (i, 0)),
                pl.BlockSpec((None, D_MODEL, tn), lambda i, j: (l, 0, j)),
                pl.BlockSpec((tm, tn), lambda i, j: (i, j))]
    x_spec = pl.BlockSpec((tm, tn), lambda i, j: (i, j))
    x_shape = jax.ShapeDtypeStruct((m, D_MODEL), F32)
    common = dict(grid=(m // tm, D_MODEL // tn), compiler_params=_params(("parallel", "arbitrary")))
    if g_next is None:
        return pl.pallas_call(_out_kernel, out_shape=x_shape, in_specs=in_specs, out_specs=x_spec,
                              name="outproj", **common)(merged, wo, x)
    return pl.pallas_call(
        _out_prenorm_kernel,
        out_shape=(x_shape, jax.ShapeDtypeStruct((m, D_MODEL), BF16),
                   jax.ShapeDtypeStruct((m, STAT_LANES), F32)),
        in_specs=in_specs + [pl.BlockSpec((1, tn), lambda i, j: (0, j))],
        out_specs=(x_spec, pl.BlockSpec((tm, tn), lambda i, j: (i, j)),
                   pl.BlockSpec((tm, STAT_LANES), lambda i, j: (i, 0))),
        name="outproj_prenorm", **common)(merged, wo, x, g_next.reshape(1, D_MODEL))


def _na_bias_table(rpb):
    cols = jnp.arange(GRID_W)
    cs = jnp.clip(cols - NA_COLS // 2, 0, GRID_W - NA_COLS)
    col_ok = (cols[None, :] >= cs[:, None]) & (cols[None, :] < cs[:, None] + NA_COLS)
    dc = jnp.clip(cols[None, :] - cols[:, None] + NA_COLS - 1, 0, 2 * NA_COLS - 2)
    toe = jnp.zeros(rpb.shape[:3] + (GRID_W, GRID_W), F32)
    for c in range(2 * NA_COLS - 1):
        toe = jnp.where(dc == c, rpb[:, :, :, c, None, None], toe)
    toe = jnp.where(col_ok, toe * LOG2E, NEG_BIAS)
    return jnp.stack(
        [jnp.concatenate([toe[:, :, n - d + NA_ROWS - 1] for n in range(NA_ROWS)], axis=-1)
         for d in range(NA_ROWS)], axis=2)


def _trunk(x, batch, seq, p):
    xg, ssq = _prenorm(x, p["norm_g"][0])
    for l in range(DEPTH):
        pab, lr = _proj_in(xg, ssq, p["w_in_t"], l, AB_COLS, lr_row=AB_COLS)
        pc = _proj_in(xg, ssq, p["w_c_t"], l, C_COLS)
        gates = _proj_gate(xg, ssq, p["w_gate"], p["b_gate"], l)
        ya = _sgu(pab, p["sgu_ln_g"][l], p["sgu_w"][l], p["sgu_b"][l])
        yb = _gla(pab, lr, p["wgk"][l], p["bgk"][l], p["gla_norm_g"][l], batch, seq)
        yc = _na(pc, p["na_bias"][l], batch, seq)
        merged = _merge(ya, yb, yc, p["w_branch"], l, gates)
        if l + 1 < DEPTH:
            x, xg, ssq = _out(merged, p["w_out"], l, x, g_next=p["norm_g"][l + 1])
        else:
            x = _out(merged, p["w_out"], l, x)
    return _rmsnorm(x, p["final_norm_g"], F32)


def kernel(x_prompt, x_sample, norm_g, w_in, sgu_ln_g, sgu_w, sgu_b, gla_w_gk, gla_b_gk, gla_norm_g,
           na_rpb, w_gate, b_gate, w_branch, w_out, final_norm_g):
    w_in_t = jnp.swapaxes(w_in, 1, 2).astype(BF16)
    wgk = jnp.zeros((DEPTH, 2, LR_PAD, GLA_KEY_W), F32)
    wgk = wgk.at[:, 0, :GLA_LOW_RANK].set(gla_w_gk[:, 0])
    wgk = wgk.at[:, 1, GLA_LOW_RANK:2 * GLA_LOW_RANK].set(gla_w_gk[:, 1])
    p = {
        "norm_g": norm_g,
        "final_norm_g": final_norm_g,
        "w_in_t": w_in_t,
        "w_c_t": w_in_t[:, AB_COLS + 2 * GLA_LOW_RANK:],
        "w_gate": w_gate.astype(BF16),
        "b_gate": b_gate.reshape(DEPTH, 1, N_BRANCH * D_MODEL),
        "sgu_ln_g": sgu_ln_g.reshape(DEPTH, 1, BRANCH_W),
        "sgu_w": sgu_w.astype(BF16),
        "sgu_b": jnp.repeat(jnp.swapaxes(sgu_b, 1, 2), SGU_GROUP_W, axis=2),
        "wgk": wgk.astype(BF16),
        "bgk": gla_b_gk.reshape(DEPTH, 2, 1, GLA_KEY_W),
        "gla_norm_g": gla_norm_g.reshape(DEPTH, 1, GLA_HEAD_V),
        "na_bias": _na_bias_table(na_rpb),
        "w_branch": w_branch.astype(BF16),
        "w_out": w_out.astype(BF16),
    }
    bp, tp, _ = x_prompt.shape
    bs, ts, _ = x_sample.shape
    y_prompt = _trunk(x_prompt.reshape(bp * tp, D_MODEL), bp, tp, p).reshape(x_prompt.shape)
    y_sample = _trunk(x_sample.reshape(bs * ts, D_MODEL), bs, ts, p).reshape(x_sample.shape)
    return (y_prompt, y_sample)
```

```python
import functools

import jax
import jax.numpy as jnp
from jax import lax
from jax.experimental import pallas as pl
from jax.experimental.pallas import tpu as pltpu

F32 = jnp.float32
BF16 = jnp.bfloat16

D_MODEL = 4096
DEPTH = 4
BRANCH_W = D_MODEL // 2
N_BRANCH = 3
GRID_W = 64
SGU_CHUNK = 128
SGU_GROUPS = 8
SGU_GROUP_W = BRANCH_W // SGU_GROUPS
GLA_HEADS = 4
GLA_KEY_W = BRANCH_W // 2
GLA_HEAD_K = GLA_KEY_W // GLA_HEADS
GLA_HEAD_V = BRANCH_W // GLA_HEADS
GLA_LOW_RANK = 16
GLA_GATE_TEMP = 16.0
GLA_CHUNK = 64
NA_HEAD_DIM = 128
NA_HEADS = BRANCH_W // NA_HEAD_DIM
NA_ROWS = 8
NA_COLS = 16
EPS = 1e-6
LOG2E = 1.4426950408889634

COL_A_U = 0
COL_A_V = COL_A_U + BRANCH_W
COL_A_Z = COL_A_V + BRANCH_W
COL_B_Q = COL_A_Z + BRANCH_W
COL_B_K = COL_B_Q + GLA_KEY_W
COL_B_V = COL_B_K + GLA_KEY_W
COL_B_Z = COL_B_V + BRANCH_W
AB_COLS = COL_B_Z + BRANCH_W
COL_C_Q = 0
COL_C_K = COL_C_Q + BRANCH_W
COL_C_V = COL_C_K + BRANCH_W
COL_C_Z = COL_C_V + BRANCH_W
C_COLS = COL_C_Z + BRANCH_W
LR_PAD = 128
STAT_LANES = 128

VMEM_LIMIT = 56 * 1024 * 1024
NEG_BIAS = -1e30

PROJ_TM, PROJ_TN = 1024, 1024
PROJ_CHUNK = 256
NORM_TM = 256
SGU_TM = 512
GLA_TB = 256
GLA_SKEW = 0
NA_GROUP = 32
MERGE_TM, MERGE_TN = 1024, 512
OUT_TM, OUT_TN = 1024, 1024


def _params(sem):
    return pltpu.CompilerParams(dimension_semantics=sem, vmem_limit_bytes=VMEM_LIMIT)


def _silu(z):
    return z * jax.nn.sigmoid(z)


def _rmsnorm_kernel(x_ref, g_ref, o_ref):
    x = x_ref[...]
    ms = jnp.mean(x * x, axis=-1, keepdims=True)
    o_ref[...] = (x * lax.rsqrt(ms + EPS) * g_ref[...]).astype(o_ref.dtype)


def _rmsnorm(x, g, out_dtype):
    m = x.shape[0]
    return pl.pallas_call(
        _rmsnorm_kernel,
        out_shape=jax.ShapeDtypeStruct((m, D_MODEL), out_dtype),
        grid=(m // NORM_TM,),
        in_specs=[pl.BlockSpec((NORM_TM, D_MODEL), lambda i: (i, 0)),
                  pl.BlockSpec((1, D_MODEL), lambda i: (0, 0))],
        out_specs=pl.BlockSpec((NORM_TM, D_MODEL), lambda i: (i, 0)),
        compiler_params=_params(("parallel",)),
        name="rmsnorm",
    )(x, g.reshape(1, D_MODEL))


def _prenorm_kernel(x_ref, g_ref, xg_ref, ssq_ref):
    x = x_ref[...]
    xg_ref[...] = (x * g_ref[...]).astype(xg_ref.dtype)
    ssq_ref[...] = jnp.broadcast_to(jnp.sum(x * x, axis=-1, keepdims=True), ssq_ref.shape)


def _prenorm(x, g):
    m = x.shape[0]
    return pl.pallas_call(
        _prenorm_kernel,
        out_shape=(jax.ShapeDtypeStruct((m, D_MODEL), BF16), jax.ShapeDtypeStruct((m, STAT_LANES), F32)),
        grid=(m // NORM_TM,),
        in_specs=[pl.BlockSpec((NORM_TM, D_MODEL), lambda i: (i, 0)),
                  pl.BlockSpec((1, D_MODEL), lambda i: (0, 0))],
        out_specs=(pl.BlockSpec((NORM_TM, D_MODEL), lambda i: (i, 0)),
                   pl.BlockSpec((NORM_TM, STAT_LANES), lambda i: (i, 0))),
        compiler_params=_params(("parallel",)),
        name="prenorm",
    )(x, g.reshape(1, D_MODEL))


def _rstd(ssq_ref):
    return lax.rsqrt(ssq_ref[...] * (1.0 / D_MODEL) + EPS)


_NT = (((1,), (1,)), ((), ()))
_TN = (((0,), (0,)), ((), ()))


def _dot_chunks(h_ref, w_ref, o_ref, epilogue, transposed=False):
    for c in range(o_ref.shape[1] // PROJ_CHUNK):
        cols = slice(c * PROJ_CHUNK, (c + 1) * PROJ_CHUNK)
        if transposed:
            acc = lax.dot_general(h_ref[...], w_ref[cols, :], _NT, preferred_element_type=F32)
        else:
            acc = jnp.dot(h_ref[...], w_ref[:, cols], preferred_element_type=F32)
        o_ref[:, cols] = epilogue(acc, cols).astype(o_ref.dtype)


def _proj_kernel(xg_ref, ssq_ref, wt_ref, o_ref):
    rstd = jnp.concatenate([_rstd(ssq_ref)] * (PROJ_CHUNK // STAT_LANES), axis=1)
    _dot_chunks(xg_ref, wt_ref, o_ref, lambda acc, cols: acc * rstd, transposed=True)


def _proj_lr_kernel(xg_ref, ssq_ref, wt_ref, wlrt_ref, o_ref, olr_ref):
    _proj_kernel(xg_ref, ssq_ref, wt_ref, o_ref)

    @pl.when(pl.program_id(1) == 0)
    def _():
        lr = lax.dot_general(xg_ref[...], wlrt_ref[...], _NT, preferred_element_type=F32)
        olr_ref[...] = (lr * _rstd(ssq_ref)).astype(olr_ref.dtype)


def _proj_gate_kernel(xg_ref, ssq_ref, w_ref, b_ref, o_ref):
    rstd = jnp.concatenate([_rstd(ssq_ref)] * (PROJ_CHUNK // STAT_LANES), axis=1)
    _dot_chunks(xg_ref, w_ref, o_ref, lambda acc, cols: jax.nn.sigmoid(acc * rstd + b_ref[:, cols]))


def _proj_specs(m, n):
    assert LR_PAD == STAT_LANES
    h_spec = pl.BlockSpec((PROJ_TM, D_MODEL), lambda i, j: (i, 0))
    s_spec = pl.BlockSpec((PROJ_TM, STAT_LANES), lambda i, j: (i, 0))
    o_spec = pl.BlockSpec((PROJ_TM, PROJ_TN), lambda i, j: (i, j))
    common = dict(grid=(m // PROJ_TM, n // PROJ_TN), compiler_params=_params(("parallel", "arbitrary")))
    return h_spec, s_spec, o_spec, jax.ShapeDtypeStruct((m, n), BF16), common


def _proj_in(xg, ssq, wt, l, n, lr_row=None):
    m = xg.shape[0]
    h_spec, s_spec, o_spec, o_shape, common = _proj_specs(m, n)
    wt_spec = pl.BlockSpec((None, PROJ_TN, D_MODEL), lambda i, j: (l, j, 0))
    if lr_row is None:
        return pl.pallas_call(_proj_kernel, out_shape=o_shape, in_specs=[h_spec, s_spec, wt_spec],
                              out_specs=o_spec, name="proj_c", **common)(xg, ssq, wt)
    return pl.pallas_call(
        _proj_lr_kernel,
        out_shape=(o_shape, jax.ShapeDtypeStruct((m, LR_PAD), BF16)),
        in_specs=[h_spec, s_spec, wt_spec,
                  pl.BlockSpec((None, LR_PAD, D_MODEL), lambda i, j: (l, lr_row // LR_PAD, 0))],
        out_specs=(o_spec, pl.BlockSpec((PROJ_TM, LR_PAD), lambda i, j: (i, 0))),
        name="proj_ab", **common)(xg, ssq, wt, wt)


def _proj_gate(xg, ssq, w, b, l):
    m = xg.shape[0]
    n = w.shape[2]
    h_spec, s_spec, o_spec, o_shape, common = _proj_specs(m, n)
    return pl.pallas_call(
        _proj_gate_kernel, out_shape=o_shape,
        in_specs=[h_spec, s_spec, pl.BlockSpec((None, D_MODEL, PROJ_TN), lambda i, j: (l, 0, j)),
                  pl.BlockSpec((None, 1, PROJ_TN), lambda i, j: (l, 0, j))],
        out_specs=o_spec, name="proj_gate", **common)(xg, ssq, w, b)


def _sgu_kernel(u_ref, v_ref, z_ref, lng_ref, ws_ref, bs_ref, o_ref, vn_ref):
    v = v_ref[...].astype(F32)
    mu = jnp.mean(v, axis=-1, keepdims=True)
    xc = v - mu
    var = jnp.mean(xc * xc, axis=-1, keepdims=True)
    vn_ref[...] = (xc * lax.rsqrt(var + EPS) * lng_ref[...]).astype(vn_ref.dtype)
    for c in range(SGU_TM // SGU_CHUNK):
        rows = slice(c * SGU_CHUNK, (c + 1) * SGU_CHUNK)
        for g in range(SGU_GROUPS):
            cols = slice(g * SGU_GROUP_W, (g + 1) * SGU_GROUP_W)
            s = jnp.dot(ws_ref[g], vn_ref[rows, cols], preferred_element_type=F32) + bs_ref[:, cols]
            u = u_ref[rows, cols].astype(F32)
            z = z_ref[rows, cols].astype(F32)
            o_ref[rows, cols] = (u * s * _silu(z)).astype(o_ref.dtype)


def _sgu(proj, ln_g, ws, bs_full):
    m = proj.shape[0]
    return pl.pallas_call(
        _sgu_kernel,
        out_shape=jax.ShapeDtypeStruct((m, BRANCH_W), BF16),
        grid=(m // SGU_TM,),
        in_specs=[pl.BlockSpec((SGU_TM, BRANCH_W), lambda i: (i, COL_A_U // BRANCH_W)),
                  pl.BlockSpec((SGU_TM, BRANCH_W), lambda i: (i, COL_A_V // BRANCH_W)),
                  pl.BlockSpec((SGU_TM, BRANCH_W), lambda i: (i, COL_A_Z // BRANCH_W)),
                  pl.BlockSpec((1, BRANCH_W), lambda i: (0, 0)),
                  pl.BlockSpec((SGU_GROUPS, SGU_CHUNK, SGU_CHUNK), lambda i: (0, 0, 0)),
                  pl.BlockSpec((SGU_CHUNK, BRANCH_W), lambda i: (0, 0))],
        out_specs=pl.BlockSpec((SGU_TM, BRANCH_W), lambda i: (i, 0)),
        scratch_shapes=[pltpu.VMEM((SGU_TM, BRANCH_W), BF16)],
        compiler_params=_params(("parallel",)),
        name="sgu",
    )(proj, proj, proj, ln_g, ws, bs_full)


def _split3_dot(mat, x):
    x1 = x.astype(BF16)
    r1 = x - x1.astype(F32)
    x2 = r1.astype(BF16)
    x3 = (r1 - x2.astype(F32)).astype(BF16)
    return (jnp.dot(mat, x1, preferred_element_type=F32)
            + jnp.dot(mat, x2, preferred_element_type=F32)
            + jnp.dot(mat, x3, preferred_element_type=F32))


def _gla_masks():
    ri = jnp.arange(GLA_TB)[:, None]
    ci = jnp.arange(GLA_TB)[None, :]
    rc, cc = ri // GLA_CHUNK, ci // GLA_CHUNK
    same = rc == cc
    return jnp.stack([same & (ci <= ri), same & (ci >= ri),
                      (rc == cc + 1) & (rc % 2 == 1), (rc == cc - 1) & (rc % 2 == 0)]).astype(F32)


def _gla_kernel(qf_ref, kf_ref, vf_ref, lrf_ref, qb_ref, kb_ref, vb_ref, lrb_ref, z_ref, wgk_ref,
                bgk_ref, ng_ref, tri_ref, trib_ref, o_ref, st_ref, part_ref, *, nb):
    j = pl.program_id(2)
    tb = GLA_TB
    nchunk = tb // GLA_CHUNK
    npair = nchunk // 2
    dirs = (0, 1)

    @pl.when(j == 0)
    def _():
        st_ref[...] = jnp.zeros_like(st_ref)

    q_refs, k_refs, v_refs, lr_refs = (qf_ref, qb_ref), (kf_ref, kb_ref), (vf_ref, vb_ref), (lrf_ref, lrb_ref)
    rows = [slice(c * GLA_CHUNK, (c + 1) * GLA_CHUNK) for c in range(nchunk)]
    pairs = ([(2 * t, 2 * t + 1) for t in range(npair)],
             [(2 * t + 1, 2 * t) for t in range(npair - 1, -1, -1)])
    last = (GLA_CHUNK - 1, 0)

    def pair_rows(a, b):
        return slice(min(a, b) * GLA_CHUNK, (min(a, b) + 2) * GLA_CHUNK)

    s = [dict(pieces=[None] * npair), dict(pieces=[None] * npair)]

    def gate(d):
        x = jnp.dot(lr_refs[d][...], wgk_ref[d], preferred_element_type=F32) + bgk_ref[d]
        s[d]["g"] = (jnp.minimum(x, 0.0) - jnp.log(1.0 + jnp.exp(-jnp.abs(x)))) * (1.0 / GLA_GATE_TEMP)

    def cumsum(d):
        s[d]["gc"] = _split3_dot(trib_ref[d], s[d]["g"])

    def decays(d):
        gc = s[d]["gc"]
        gl = [gc[c * GLA_CHUNK + last[d]:c * GLA_CHUNK + last[d] + 1, :] for c in range(nchunk)]
        gt = jnp.concatenate([jnp.broadcast_to(t, (GLA_CHUNK, GLA_HEAD_K)) for t in gl], axis=0)
        q = q_refs[d][...].astype(F32) * (GLA_HEAD_K ** -0.5)
        k = k_refs[d][...].astype(F32)
        qe32 = q * jnp.exp(gc)
        kd32 = k * jnp.exp(gt - gc)
        q_e = qe32.astype(BF16)
        k_d = kd32.astype(BF16)
        q_in, k_s = [None] * nchunk, [None] * nchunk
        for a, b in pairs[d]:
            q_in[a] = q_e[rows[a]]
            q_in[b] = (qe32[rows[b]] * jnp.exp(gl[a])).astype(BF16)
            k_s[a] = (kd32[rows[a]] * jnp.exp(gl[b])).astype(BF16)
            k_s[b] = k_d[rows[b]]
        s[d].update(gl=gl, v=v_refs[d][...], q_e=q_e, k_d=k_d,
                    k_e=(k * jnp.exp(-gc)).astype(BF16),
                    q_in=jnp.concatenate(q_in, axis=0), k_s=jnp.concatenate(k_s, axis=0))

    def scores(d):
        a_in = lax.dot_general(s[d]["q_e"], s[d]["k_e"], _NT, preferred_element_type=F32)
        a_ab = lax.dot_general(s[d]["q_e"], s[d]["k_d"], _NT, preferred_element_type=F32)
        att = jnp.where(tri_ref[d] > 0.5, a_in, jnp.where(tri_ref[2 + d] > 0.5, a_ab, 0.0))
        s[d]["att"] = att.astype(BF16)

    def intra(d):
        s[d]["o_intra"] = jnp.dot(s[d]["att"], s[d]["v"], preferred_element_type=F32)

    def update(d, t):
        r = pair_rows(*pairs[d][t])
        s[d]["u"] = lax.dot_general(s[d]["v"][r], s[d]["k_s"][r], _TN, preferred_element_type=F32)

    def pair_step(d, t):
        a, b = pairs[d][t]
        r = pair_rows(a, b)
        u_cur = s[d]["u"]
        if t + 1 < npair:
            update(d, t + 1)
        st = st_ref[d]
        s[d]["pieces"][min(a, b) // 2] = s[d]["o_intra"][r] + lax.dot_general(
            s[d]["q_in"][r], st.astype(BF16), _NT, preferred_element_type=F32)
        st_ref[d] = st * jnp.exp(s[d]["gl"][a] + s[d]["gl"][b]) + u_cur

    stages = [gate, cumsum, decays, scores, intra, lambda d: update(d, 0)]
    stages += [functools.partial(lambda t, d: pair_step(d, t), t) for t in range(npair)]
    for t in range(len(stages) + GLA_SKEW):
        if t < len(stages):
            stages[t](0)
        if 0 <= t - GLA_SKEW < len(stages):
            stages[t - GLA_SKEW](1)
    outs = [jnp.concatenate(s[d]["pieces"], axis=0) for d in dirs]
    toks = (pl.multiple_of(j * tb, tb), pl.multiple_of((nb - 1 - j) * tb, tb))

    @pl.when(j < nb // 2)
    def _():
        for d in dirs:
            part_ref[pl.ds(toks[d], tb), :] = outs[d]

    @pl.when(j >= nb // 2)
    def _():
        for d in dirs:
            tot = part_ref[pl.ds(toks[d], tb), :] + outs[d]
            ms = jnp.mean(tot * tot, axis=-1, keepdims=True)
            y = tot * lax.rsqrt(ms + EPS) * ng_ref[...]
            z = z_ref[pl.ds(toks[d], tb), :].astype(F32)
            o_ref[pl.ds(toks[d], tb), :] = (y * _silu(z)).astype(o_ref.dtype)


def _gla(proj, lr, wgk_pad, bgk, norm_g, batch, seq):
    m = proj.shape[0]
    nb = seq // GLA_TB
    assert nb % 2 == 0
    kq, kv = GLA_HEAD_K, GLA_HEAD_V
    masks = _gla_masks()

    def fwd(b, j):
        return b * nb + j

    def bwd(b, j):
        return b * nb + nb - 1 - j

    def blocks(tok):
        return [pl.BlockSpec((GLA_TB, kq), lambda b, h, j: (tok(b, j), COL_B_Q // kq + h)),
                pl.BlockSpec((GLA_TB, kq), lambda b, h, j: (tok(b, j), COL_B_K // kq + h)),
                pl.BlockSpec((GLA_TB, kv), lambda b, h, j: (tok(b, j), COL_B_V // kv + h)),
                pl.BlockSpec((GLA_TB, LR_PAD), lambda b, h, j: (tok(b, j), 0))]

    return pl.pallas_call(
        functools.partial(_gla_kernel, nb=nb),
        out_shape=jax.ShapeDtypeStruct((m, BRANCH_W), BF16),
        grid=(batch, GLA_HEADS, nb),
        in_specs=blocks(fwd) + blocks(bwd) + [
            pl.BlockSpec((seq, kv), lambda b, h, j: (b, COL_B_Z // kv + h)),
            pl.BlockSpec((2, LR_PAD, kq), lambda b, h, j: (0, 0, h)),
            pl.BlockSpec((2, 1, kq), lambda b, h, j: (0, 0, h)),
            pl.BlockSpec((1, kv), lambda b, h, j: (0, 0)),
            pl.BlockSpec((4, GLA_TB, GLA_TB), lambda b, h, j: (0, 0, 0)),
            pl.BlockSpec((2, GLA_TB, GLA_TB), lambda b, h, j: (0, 0, 0))],
        out_specs=pl.BlockSpec((seq, kv), lambda b, h, j: (b, h)),
        scratch_shapes=[pltpu.VMEM((2, kv, kq), F32), pltpu.VMEM((seq, kv), F32)],
        compiler_params=_params(("parallel", "parallel", "arbitrary")),
        name="gla",
    )(proj, proj, proj, lr, proj, proj, proj, lr, proj, wgk_pad, bgk, norm_g, masks, masks[:2].astype(BF16))


def _na_kernel(q_ref, k_ref, v_ref, z_ref, bias_ref, o_ref, *, rows):
    win = NA_ROWS * GRID_W
    scale = NA_HEAD_DIM ** -0.5

    def group(g, carry):
        q0s, k0s, scores = [], [], []
        for i in range(NA_GROUP):
            r = g * NA_GROUP + i
            rs = jnp.clip(r - NA_ROWS // 2, 0, rows - NA_ROWS)
            q0 = pl.multiple_of(r * GRID_W, GRID_W)
            k0 = pl.multiple_of(rs * GRID_W, GRID_W)
            s = lax.dot_general(q_ref[pl.ds(q0, GRID_W), :], k_ref[pl.ds(k0, win), :], _NT,
                                preferred_element_type=F32)
            scores.append(s * (scale * LOG2E) + bias_ref[0, r - rs])
            q0s.append(q0)
            k0s.append(k0)
        probs, inv = [], []
        for s in scores:
            p = jnp.exp2(s - jnp.max(s, axis=-1, keepdims=True))
            inv.append(1.0 / jnp.sum(p, axis=-1, keepdims=True))
            probs.append(p.astype(BF16))
        for i in range(NA_GROUP):
            o = jnp.dot(probs[i], v_ref[pl.ds(k0s[i], win), :], preferred_element_type=F32) * inv[i]
            z = z_ref[pl.ds(q0s[i], GRID_W), :].astype(F32)
            o_ref[pl.ds(q0s[i], GRID_W), :] = (o * _silu(z)).astype(o_ref.dtype)
        return carry

    lax.fori_loop(0, rows // NA_GROUP, group, 0)


def _na(proj, bias_tab, batch, seq):
    m = proj.shape[0]
    rows = seq // GRID_W
    assert rows % NA_GROUP == 0 and rows >= NA_ROWS
    hd = NA_HEAD_DIM
    return pl.pallas_call(
        functools.partial(_na_kernel, rows=rows),
        out_shape=jax.ShapeDtypeStruct((m, BRANCH_W), BF16),
        grid=(NA_HEADS, batch),
        in_specs=[pl.BlockSpec((seq, hd), lambda h, b: (b, COL_C_Q // hd + h)),
                  pl.BlockSpec((seq, hd), lambda h, b: (b, COL_C_K // hd + h)),
                  pl.BlockSpec((seq, hd), lambda h, b: (b, COL_C_V // hd + h)),
                  pl.BlockSpec((seq, hd), lambda h, b: (b, COL_C_Z // hd + h)),
                  pl.BlockSpec((1, NA_ROWS, GRID_W, NA_ROWS * GRID_W), lambda h, b: (h, 0, 0, 0))],
        out_specs=pl.BlockSpec((seq, hd), lambda h, b: (b, h)),
        compiler_params=_params(("parallel", "parallel")),
        name="na",
    )(proj, proj, proj, proj, bias_tab)


def _merge_kernel(ya_ref, yb_ref, yc_ref, wb_ref, ga_ref, gb_ref, gc_ref, o_ref):
    y_refs = (ya_ref, yb_ref, yc_ref)
    g_refs = (ga_ref, gb_ref, gc_ref)
    for c in range(o_ref.shape[1] // PROJ_CHUNK):
        cols = slice(c * PROJ_CHUNK, (c + 1) * PROJ_CHUNK)
        acc = None
        for b in range(N_BRANCH):
            t = g_refs[b][:, cols].astype(F32) * jnp.dot(y_refs[b][...], wb_ref[b, :, cols],
                                                         preferred_element_type=F32)
            acc = t if acc is None else acc + t
        o_ref[:, cols] = acc.astype(o_ref.dtype)


def _merge(ya, yb, yc, wb, l, gates):
    m = ya.shape[0]
    tm, tn = MERGE_TM, MERGE_TN
    y_spec = pl.BlockSpec((tm, BRANCH_W), lambda i, j: (i, 0))

    def gate_spec(b):
        return pl.BlockSpec((tm, tn), lambda i, j: (i, b * D_MODEL // tn + j))

    return pl.pallas_call(
        _merge_kernel,
        out_shape=jax.ShapeDtypeStruct((m, D_MODEL), BF16),
        grid=(m // tm, D_MODEL // tn),
        in_specs=[y_spec, y_spec, y_spec,
                  pl.BlockSpec((None, N_BRANCH, BRANCH_W, tn), lambda i, j: (l, 0, 0, j)),
                  gate_spec(0), gate_spec(1), gate_spec(2)],
        out_specs=pl.BlockSpec((tm, tn), lambda i, j: (i, j)),
        compiler_params=_params(("parallel", "arbitrary")),
        name="merge",
    )(ya, yb, yc, wb, gates, gates, gates)


def _out_kernel(m_ref, w_ref, x_ref, o_ref):
    _dot_chunks(m_ref, w_ref, o_ref, lambda acc, cols: x_ref[:, cols] + acc)


def _out_prenorm_kernel(m_ref, w_ref, x_ref, g_ref, o_ref, xg_ref, ssq_ref):
    @pl.when(pl.program_id(1) == 0)
    def _():
        ssq_ref[...] = jnp.zeros_like(ssq_ref)

    ssq = None
    for c in range(o_ref.shape[1] // PROJ_CHUNK):
        cols = slice(c * PROJ_CHUNK, (c + 1) * PROJ_CHUNK)
        xn = x_ref[:, cols] + jnp.dot(m_ref[...], w_ref[:, cols], preferred_element_type=F32)
        o_ref[:, cols] = xn
        xg_ref[:, cols] = (xn * g_ref[:, cols]).astype(xg_ref.dtype)
        s = jnp.sum(xn * xn, axis=-1, keepdims=True)
        ssq = s if ssq is None else ssq + s
    ssq_ref[...] += jnp.broadcast_to(ssq, ssq_ref.shape)


def _out(merged, wo, l, x, g_next=None):
    m = x.shape[0]
    tm, tn = OUT_TM, OUT_TN
    in_specs = [pl.BlockSpec((tm, D_MODEL), lambda i, j: (i, 0)),
                pl.BlockSpec((None, D_MODEL, tn), lambda i, j: (l, 0, j)),
                pl.BlockSpec((tm, tn), lambda i, j: (i, j))]
    x_spec = pl.BlockSpec((tm, tn), lambda i, j: (i, j))
    x_shape = jax.ShapeDtypeStruct((m, D_MODEL), F32)
    common = dict(grid=(m // tm, D_MODEL // tn), compiler_params=_params(("parallel", "arbitrary")))
    if g_next is None:
        return pl.pallas_call(_out_kernel, out_shape=x_shape, in_specs=in_specs, out_specs=x_spec,
                              name="outproj", **common)(merged, wo, x)
    return pl.pallas_call(
        _out_prenorm_kernel,
        out_shape=(x_shape, jax.ShapeDtypeStruct((m, D_MODEL), BF16),
                   jax.ShapeDtypeStruct((m, STAT_LANES), F32)),
        in_specs=in_specs + [pl.BlockSpec((1, tn), lambda i, j: (0, j))],
        out_specs=(x_spec, pl.BlockSpec((tm, tn), lambda i, j: (i, j)),
                   pl.BlockSpec((tm, STAT_LANES), lambda i, j: (i, 0))),
        name="outproj_prenorm", **common)(merged, wo, x, g_next.reshape(1, D_MODEL))


def _na_bias_table(rpb):
    cols = jnp.arange(GRID_W)
    cs = jnp.clip(cols - NA_COLS // 2, 0, GRID_W - NA_COLS)
    col_ok = (cols[None, :] >= cs[:, None]) & (cols[None, :] < cs[:, None] + NA_COLS)
    dc = jnp.clip(cols[None, :] - cols[:, None] + NA_COLS - 1, 0, 2 * NA_COLS - 2)
    toe = jnp.zeros(rpb.shape[:3] + (GRID_W, GRID_W), F32)
    for c in range(2 * NA_COLS - 1):
        toe = jnp.where(dc == c, rpb[:, :, :, c, None, None], toe)
    toe = jnp.where(col_ok, toe * LOG2E, NEG_BIAS)
    return jnp.stack(
        [jnp.concatenate([toe[:, :, n - d + NA_ROWS - 1] for n in range(NA_ROWS)], axis=-1)
         for d in range(NA_ROWS)], axis=2)


def _trunk(x, batch, seq, p):
    xg, ssq = _prenorm(x, p["norm_g"][0])
    for l in range(DEPTH):
        pab, lr = _proj_in(xg, ssq, p["w_in_t"], l, AB_COLS, lr_row=AB_COLS)
        pc = _proj_in(xg, ssq, p["w_c_t"], l, C_COLS)
        gates = _proj_gate(xg, ssq, p["w_gate"], p["b_gate"], l)
        ya = _sgu(pab, p["sgu_ln_g"][l], p["sgu_w"][l], p["sgu_b"][l])
        yb = _gla(pab, lr, p["wgk"][l], p["bgk"][l], p["gla_norm_g"][l], batch, seq)
        yc = _na(pc, p["na_bias"][l], batch, seq)
        merged = _merge(ya, yb, yc, p["w_branch"], l, gates)
        if l + 1 < DEPTH:
            x, xg, ssq = _out(merged, p["w_out"], l, x, g_next=p["norm_g"][l + 1])
        else:
            x = _out(merged, p["w_out"], l, x)
    return _rmsnorm(x, p["final_norm_g"], F32)


def kernel(x_prompt, x_sample, norm_g, w_in, sgu_ln_g, sgu_w, sgu_b, gla_w_gk, gla_b_gk, gla_norm_g,
           na_rpb, w_gate, b_gate, w_branch, w_out, final_norm_g):
    w_in_t = jnp.swapaxes(w_in, 1, 2).astype(BF16)
    wgk = jnp.zeros((DEPTH, 2, LR_PAD, GLA_KEY_W), F32)
    wgk = wgk.at[:, 0, :GLA_LOW_RANK].set(gla_w_gk[:, 0])
    wgk = wgk.at[:, 1, GLA_LOW_RANK:2 * GLA_LOW_RANK].set(gla_w_gk[:, 1])
    p = {
        "norm_g": norm_g,
        "final_norm_g": final_norm_g,
        "w_in_t": w_in_t,
        "w_c_t": w_in_t[:, AB_COLS + 2 * GLA_LOW_RANK:],
        "w_gate": w_gate.astype(BF16),
        "b_gate": b_gate.reshape(DEPTH, 1, N_BRANCH * D_MODEL),
        "sgu_ln_g": sgu_ln_g.reshape(DEPTH, 1, BRANCH_W),
        "sgu_w": sgu_w.astype(BF16),
        "sgu_b": jnp.repeat(jnp.swapaxes(sgu_b, 1, 2), SGU_GROUP_W, axis=2),
        "wgk": wgk.astype(BF16),
        "bgk": gla_b_gk.reshape(DEPTH, 2, 1, GLA_KEY_W),
        "gla_norm_g": gla_norm_g.reshape(DEPTH, 1, GLA_HEAD_V),
        "na_bias": _na_bias_table(na_rpb),
        "w_branch": w_branch.astype(BF16),
        "w_out": w_out.astype(BF16),
    }
    bp, tp, _ = x_prompt.shape
    bs, ts, _ = x_sample.shape
    y_prompt = _trunk(x_prompt.reshape(bp * tp, D_MODEL), bp, tp, p).reshape(x_prompt.shape)
    y_sample = _trunk(x_sample.reshape(bs * ts, D_MODEL), bs, ts, p).reshape(x_sample.shape)
    return (y_prompt, y_sample)
```

```python
import functools

import jax
import jax.numpy as jnp
from jax import lax
from jax.experimental import pallas as pl
from jax.experimental.pallas import tpu as pltpu

F32 = jnp.float32
BF16 = jnp.bfloat16

D_MODEL = 4096
DEPTH = 4
BRANCH_W = D_MODEL // 2
N_BRANCH = 3
GRID_W = 64
SGU_CHUNK = 128
SGU_GROUPS = 8
SGU_GROUP_W = BRANCH_W // SGU_GROUPS
GLA_HEADS = 4
GLA_KEY_W = BRANCH_W // 2
GLA_HEAD_K = GLA_KEY_W // GLA_HEADS
GLA_HEAD_V = BRANCH_W // GLA_HEADS
GLA_LOW_RANK = 16
GLA_GATE_TEMP = 16.0
GLA_CHUNK = 64
NA_HEAD_DIM = 128
NA_HEADS = BRANCH_W // NA_HEAD_DIM
NA_ROWS = 8
NA_COLS = 16
EPS = 1e-6
LOG2E = 1.4426950408889634

COL_A_U = 0
COL_A_V = COL_A_U + BRANCH_W
COL_A_Z = COL_A_V + BRANCH_W
COL_B_Q = COL_A_Z + BRANCH_W
COL_B_K = COL_B_Q + GLA_KEY_W
COL_B_V = COL_B_K + GLA_KEY_W
COL_B_Z = COL_B_V + BRANCH_W
AB_COLS = COL_B_Z + BRANCH_W
COL_C_Q = 0
COL_C_K = COL_C_Q + BRANCH_W
COL_C_V = COL_C_K + BRANCH_W
COL_C_Z = COL_C_V + BRANCH_W
C_COLS = COL_C_Z + BRANCH_W
LR_PAD = 128
STAT_LANES = 128

VMEM_LIMIT = 56 * 1024 * 1024
NEG_BIAS = -1e30

PROJ_TM, PROJ_TN = 1024, 1024
PROJ_CHUNK = 256
NORM_TM = 256
SGU_TM = 512
GLA_TB = 256
GLA_SKEW = 0
NA_GROUP = 32
MERGE_TM, MERGE_TN = 1024, 512
OUT_TM, OUT_TN = 1024, 1024


def _params(sem):
    return pltpu.CompilerParams(dimension_semantics=sem, vmem_limit_bytes=VMEM_LIMIT)


def _silu(z):
    return z * jax.nn.sigmoid(z)


def _rmsnorm_kernel(x_ref, g_ref, o_ref):
    x = x_ref[...]
    ms = jnp.mean(x * x, axis=-1, keepdims=True)
    o_ref[...] = (x * lax.rsqrt(ms + EPS) * g_ref[...]).astype(o_ref.dtype)


def _rmsnorm(x, g, out_dtype):
    m = x.shape[0]
    return pl.pallas_call(
        _rmsnorm_kernel,
        out_shape=jax.ShapeDtypeStruct((m, D_MODEL), out_dtype),
        grid=(m // NORM_TM,),
        in_specs=[pl.BlockSpec((NORM_TM, D_MODEL), lambda i: (i, 0)),
                  pl.BlockSpec((1, D_MODEL), lambda i: (0, 0))],
        out_specs=pl.BlockSpec((NORM_TM, D_MODEL), lambda i: (i, 0)),
        compiler_params=_params(("parallel",)),
        name="rmsnorm",
    )(x, g.reshape(1, D_MODEL))


def _prenorm_kernel(x_ref, g_ref, xg_ref, ssq_ref):
    x = x_ref[...]
    xg_ref[...] = (x * g_ref[...]).astype(xg_ref.dtype)
    ssq_ref[...] = jnp.broadcast_to(jnp.sum(x * x, axis=-1, keepdims=True), ssq_ref.shape)


def _prenorm(x, g):
    m = x.shape[0]
    return pl.pallas_call(
        _prenorm_kernel,
        out_shape=(jax.ShapeDtypeStruct((m, D_MODEL), BF16), jax.ShapeDtypeStruct((m, STAT_LANES), F32)),
        grid=(m // NORM_TM,),
        in_specs=[pl.BlockSpec((NORM_TM, D_MODEL), lambda i: (i, 0)),
                  pl.BlockSpec((1, D_MODEL), lambda i: (0, 0))],
        out_specs=(pl.BlockSpec((NORM_TM, D_MODEL), lambda i: (i, 0)),
                   pl.BlockSpec((NORM_TM, STAT_LANES), lambda i: (i, 0))),
        compiler_params=_params(("parallel",)),
        name="prenorm",
    )(x, g.reshape(1, D_MODEL))


def _rstd(ssq_ref):
    return lax.rsqrt(ssq_ref[...] * (1.0 / D_MODEL) + EPS)


_NT = (((1,), (1,)), ((), ()))
_TN = (((0,), (0,)), ((), ()))


def _dot_chunks(h_ref, w_ref, o_ref, epilogue, transposed=False):
    for c in range(o_ref.shape[1] // PROJ_CHUNK):
        cols = slice(c * PROJ_CHUNK, (c + 1) * PROJ_CHUNK)
        if transposed:
            acc = lax.dot_general(h_ref[...], w_ref[cols, :], _NT, preferred_element_type=F32)
        else:
            acc = jnp.dot(h_ref[...], w_ref[:, cols], preferred_element_type=F32)
        o_ref[:, cols] = epilogue(acc, cols).astype(o_ref.dtype)


def _proj_kernel(xg_ref, ssq_ref, wt_ref, o_ref):
    rstd = jnp.concatenate([_rstd(ssq_ref)] * (PROJ_CHUNK // STAT_LANES), axis=1)
    _dot_chunks(xg_ref, wt_ref, o_ref, lambda acc, cols: acc * rstd, transposed=True)


def _proj_lr_kernel(xg_ref, ssq_ref, wt_ref, wlrt_ref, o_ref, olr_ref):
    _proj_kernel(xg_ref, ssq_ref, wt_ref, o_ref)

    @pl.when(pl.program_id(1) == 0)
    def _():
        lr = lax.dot_general(xg_ref[...], wlrt_ref[...], _NT, preferred_element_type=F32)
        olr_ref[...] = (lr * _rstd(ssq_ref)).astype(olr_ref.dtype)


def _proj_gate_kernel(xg_ref, ssq_ref, w_ref, b_ref, o_ref):
    rstd = jnp.concatenate([_rstd(ssq_ref)] * (PROJ_CHUNK // STAT_LANES), axis=1)
    _dot_chunks(xg_ref, w_ref, o_ref, lambda acc, cols: acc * rstd + b_ref[:, cols])


def _proj_specs(m, n):
    assert LR_PAD == STAT_LANES
    h_spec = pl.BlockSpec((PROJ_TM, D_MODEL), lambda i, j: (i, 0))
    s_spec = pl.BlockSpec((PROJ_TM, STAT_LANES), lambda i, j: (i, 0))
    o_spec = pl.BlockSpec((PROJ_TM, PROJ_TN), lambda i, j: (i, j))
    common = dict(grid=(m // PROJ_TM, n // PROJ_TN), compiler_params=_params(("parallel", "arbitrary")))
    return h_spec, s_spec, o_spec, jax.ShapeDtypeStruct((m, n), BF16), common


def _proj_in(xg, ssq, wt, l, n, lr_row=None):
    m = xg.shape[0]
    h_spec, s_spec, o_spec, o_shape, common = _proj_specs(m, n)
    wt_spec = pl.BlockSpec((None, PROJ_TN, D_MODEL), lambda i, j: (l, j, 0))
    if lr_row is None:
        return pl.pallas_call(_proj_kernel, out_shape=o_shape, in_specs=[h_spec, s_spec, wt_spec],
                              out_specs=o_spec, name="proj_c", **common)(xg, ssq, wt)
    return pl.pallas_call(
        _proj_lr_kernel,
        out_shape=(o_shape, jax.ShapeDtypeStruct((m, LR_PAD), BF16)),
        in_specs=[h_spec, s_spec, wt_spec,
                  pl.BlockSpec((None, LR_PAD, D_MODEL), lambda i, j: (l, lr_row // LR_PAD, 0))],
        out_specs=(o_spec, pl.BlockSpec((PROJ_TM, LR_PAD), lambda i, j: (i, 0))),
        name="proj_ab", **common)(xg, ssq, wt, wt)


def _proj_gate(xg, ssq, w, b, l):
    m = xg.shape[0]
    n = w.shape[2]
    h_spec, s_spec, o_spec, o_shape, common = _proj_specs(m, n)
    return pl.pallas_call(
        _proj_gate_kernel, out_shape=o_shape,
        in_specs=[h_spec, s_spec, pl.BlockSpec((None, D_MODEL, PROJ_TN), lambda i, j: (l, 0, j)),
                  pl.BlockSpec((None, 1, PROJ_TN), lambda i, j: (l, 0, j))],
        out_specs=o_spec, name="proj_gate", **common)(xg, ssq, w, b)


def _sgu_kernel(u_ref, v_ref, z_ref, lng_ref, ws_ref, bs_ref, o_ref, vn_ref):
    v = v_ref[...].astype(F32)
    mu = jnp.mean(v, axis=-1, keepdims=True)
    xc = v - mu
    var = jnp.mean(xc * xc, axis=-1, keepdims=True)
    vn_ref[...] = (xc * lax.rsqrt(var + EPS) * lng_ref[...]).astype(vn_ref.dtype)
    for c in range(SGU_TM // SGU_CHUNK):
        rows = slice(c * SGU_CHUNK, (c + 1) * SGU_CHUNK)
        for g in range(SGU_GROUPS):
            cols = slice(g * SGU_GROUP_W, (g + 1) * SGU_GROUP_W)
            s = jnp.dot(ws_ref[g], vn_ref[rows, cols], preferred_element_type=F32) + bs_ref[:, cols]
            u = u_ref[rows, cols].astype(F32)
            z = z_ref[rows, cols].astype(F32)
            o_ref[rows, cols] = (u * s * _silu(z)).astype(o_ref.dtype)


def _sgu(proj, ln_g, ws, bs_full):
    m = proj.shape[0]
    return pl.pallas_call(
        _sgu_kernel,
        out_shape=jax.ShapeDtypeStruct((m, BRANCH_W), BF16),
        grid=(m // SGU_TM,),
        in_specs=[pl.BlockSpec((SGU_TM, BRANCH_W), lambda i: (i, COL_A_U // BRANCH_W)),
                  pl.BlockSpec((SGU_TM, BRANCH_W), lambda i: (i, COL_A_V // BRANCH_W)),
                  pl.BlockSpec((SGU_TM, BRANCH_W), lambda i: (i, COL_A_Z // BRANCH_W)),
                  pl.BlockSpec((1, BRANCH_W), lambda i: (0, 0)),
                  pl.BlockSpec((SGU_GROUPS, SGU_CHUNK, SGU_CHUNK), lambda i: (0, 0, 0)),
                  pl.BlockSpec((SGU_CHUNK, BRANCH_W), lambda i: (0, 0))],
        out_specs=pl.BlockSpec((SGU_TM, BRANCH_W), lambda i: (i, 0)),
        scratch_shapes=[pltpu.VMEM((SGU_TM, BRANCH_W), BF16)],
        compiler_params=_params(("parallel",)),
        name="sgu",
    )(proj, proj, proj, ln_g, ws, bs_full)


def _split3_dot(mat, x):
    x1 = x.astype(BF16)
    r1 = x - x1.astype(F32)
    x2 = r1.astype(BF16)
    x3 = (r1 - x2.astype(F32)).astype(BF16)
    return (jnp.dot(mat, x1, preferred_element_type=F32)
            + jnp.dot(mat, x2, preferred_element_type=F32)
            + jnp.dot(mat, x3, preferred_element_type=F32))


def _gla_masks():
    ri = jnp.arange(GLA_TB)[:, None]
    ci = jnp.arange(GLA_TB)[None, :]
    rc, cc = ri // GLA_CHUNK, ci // GLA_CHUNK
    same = rc == cc
    return jnp.stack([same & (ci <= ri), same & (ci >= ri),
                      (rc == cc + 1) & (rc % 2 == 1), (rc == cc - 1) & (rc % 2 == 0)]).astype(F32)


def _gla_kernel(qf_ref, kf_ref, vf_ref, lrf_ref, qb_ref, kb_ref, vb_ref, lrb_ref, z_ref, wgk_ref,
                bgk_ref, ng_ref, tri_ref, trib_ref, o_ref, st_ref, part_ref, *, nb):
    j = pl.program_id(2)
    tb = GLA_TB
    nchunk = tb // GLA_CHUNK
    npair = nchunk // 2
    dirs = (0, 1)

    @pl.when(j == 0)
    def _():
        st_ref[...] = jnp.zeros_like(st_ref)

    q_refs, k_refs, v_refs, lr_refs = (qf_ref, qb_ref), (kf_ref, kb_ref), (vf_ref, vb_ref), (lrf_ref, lrb_ref)
    rows = [slice(c * GLA_CHUNK, (c + 1) * GLA_CHUNK) for c in range(nchunk)]
    pairs = ([(2 * t, 2 * t + 1) for t in range(npair)],
             [(2 * t + 1, 2 * t) for t in range(npair - 1, -1, -1)])
    last = (GLA_CHUNK - 1, 0)

    def pair_rows(a, b):
        return slice(min(a, b) * GLA_CHUNK, (min(a, b) + 2) * GLA_CHUNK)

    s = [dict(pieces=[None] * npair), dict(pieces=[None] * npair)]

    def gate(d):
        x = jnp.dot(lr_refs[d][...], wgk_ref[d], preferred_element_type=F32) + bgk_ref[d]
        s[d]["g"] = (jnp.minimum(x, 0.0) - jnp.log(1.0 + jnp.exp(-jnp.abs(x)))) * (1.0 / GLA_GATE_TEMP)

    def cumsum(d):
        s[d]["gc"] = _split3_dot(trib_ref[d], s[d]["g"])

    def decays(d):
        gc = s[d]["gc"]
        gl = [gc[c * GLA_CHUNK + last[d]:c * GLA_CHUNK + last[d] + 1, :] for c in range(nchunk)]
        gt = jnp.concatenate([jnp.broadcast_to(t, (GLA_CHUNK, GLA_HEAD_K)) for t in gl], axis=0)
        q = q_refs[d][...].astype(F32) * (GLA_HEAD_K ** -0.5)
        k = k_refs[d][...].astype(F32)
        qe32 = q * jnp.exp(gc)
        kd32 = k * jnp.exp(gt - gc)
        q_e = qe32.astype(BF16)
        k_d = kd32.astype(BF16)
        q_in, k_s = [None] * nchunk, [None] * nchunk
        for a, b in pairs[d]:
            q_in[a] = q_e[rows[a]]
            q_in[b] = (qe32[rows[b]] * jnp.exp(gl[a])).astype(BF16)
            k_s[a] = (kd32[rows[a]] * jnp.exp(gl[b])).astype(BF16)
            k_s[b] = k_d[rows[b]]
        s[d].update(gl=gl, v=v_refs[d][...], q_e=q_e, k_d=k_d,
                    k_e=(k * jnp.exp(-gc)).astype(BF16),
                    q_in=jnp.concatenate(q_in, axis=0), k_s=jnp.concatenate(k_s, axis=0))

    def scores(d):
        a_in = lax.dot_general(s[d]["q_e"], s[d]["k_e"], _NT, preferred_element_type=F32)
        a_ab = lax.dot_general(s[d]["q_e"], s[d]["k_d"], _NT, preferred_element_type=F32)
        att = jnp.where(tri_ref[d] > 0.5, a_in, jnp.where(tri_ref[2 + d] > 0.5, a_ab, 0.0))
        s[d]["att"] = att.astype(BF16)

    def intra(d):
        s[d]["o_intra"] = jnp.dot(s[d]["att"], s[d]["v"], preferred_element_type=F32)

    def update(d, t):
        r = pair_rows(*pairs[d][t])
        s[d]["u"] = lax.dot_general(s[d]["v"][r], s[d]["k_s"][r], _TN, preferred_element_type=F32)

    def pair_step(d, t):
        a, b = pairs[d][t]
        r = pair_rows(a, b)
        u_cur = s[d]["u"]
        if t + 1 < npair:
            update(d, t + 1)
        st = st_ref[d]
        s[d]["pieces"][min(a, b) // 2] = s[d]["o_intra"][r] + lax.dot_general(
            s[d]["q_in"][r], st.astype(BF16), _NT, preferred_element_type=F32)
        st_ref[d] = st * jnp.exp(s[d]["gl"][a] + s[d]["gl"][b]) + u_cur

    stages = [gate, cumsum, decays, scores, intra, lambda d: update(d, 0)]
    stages += [functools.partial(lambda t, d: pair_step(d, t), t) for t in range(npair)]
    for t in range(len(stages) + GLA_SKEW):
        if t < len(stages):
            stages[t](0)
        if 0 <= t - GLA_SKEW < len(stages):
            stages[t - GLA_SKEW](1)
    outs = [jnp.concatenate(s[d]["pieces"], axis=0) for d in dirs]
    toks = (pl.multiple_of(j * tb, tb), pl.multiple_of((nb - 1 - j) * tb, tb))

    @pl.when(j < nb // 2)
    def _():
        for d in dirs:
            part_ref[pl.ds(toks[d], tb), :] = outs[d]

    @pl.when(j >= nb // 2)
    def _():
        for d in dirs:
            tot = part_ref[pl.ds(toks[d], tb), :] + outs[d]
            ms = jnp.mean(tot * tot, axis=-1, keepdims=True)
            y = tot * lax.rsqrt(ms + EPS) * ng_ref[...]
            z = z_ref[pl.ds(toks[d], tb), :].astype(F32)
            o_ref[pl.ds(toks[d], tb), :] = (y * _silu(z)).astype(o_ref.dtype)


def _gla(proj, lr, wgk_pad, bgk, norm_g, batch, seq):
    m = proj.shape[0]
    nb = seq // GLA_TB
    assert nb % 2 == 0
    kq, kv = GLA_HEAD_K, GLA_HEAD_V
    masks = _gla_masks()

    def fwd(b, j):
        return b * nb + j

    def bwd(b, j):
        return b * nb + nb - 1 - j

    def blocks(tok):
        return [pl.BlockSpec((GLA_TB, kq), lambda b, h, j: (tok(b, j), COL_B_Q // kq + h)),
                pl.BlockSpec((GLA_TB, kq), lambda b, h, j: (tok(b, j), COL_B_K // kq + h)),
                pl.BlockSpec((GLA_TB, kv), lambda b, h, j: (tok(b, j), COL_B_V // kv + h)),
                pl.BlockSpec((GLA_TB, LR_PAD), lambda b, h, j: (tok(b, j), 0))]

    return pl.pallas_call(
        functools.partial(_gla_kernel, nb=nb),
        out_shape=jax.ShapeDtypeStruct((m, BRANCH_W), BF16),
        grid=(batch, GLA_HEADS, nb),
        in_specs=blocks(fwd) + blocks(bwd) + [
            pl.BlockSpec((seq, kv), lambda b, h, j: (b, COL_B_Z // kv + h)),
            pl.BlockSpec((2, LR_PAD, kq), lambda b, h, j: (0, 0, h)),
            pl.BlockSpec((2, 1, kq), lambda b, h, j: (0, 0, h)),
            pl.BlockSpec((1, kv), lambda b, h, j: (0, 0)),
            pl.BlockSpec((4, GLA_TB, GLA_TB), lambda b, h, j: (0, 0, 0)),
            pl.BlockSpec((2, GLA_TB, GLA_TB), lambda b, h, j: (0, 0, 0))],
        out_specs=pl.BlockSpec((seq, kv), lambda b, h, j: (b, h)),
        scratch_shapes=[pltpu.VMEM((2, kv, kq), F32), pltpu.VMEM((seq, kv), F32)],
        compiler_params=_params(("parallel", "parallel", "arbitrary")),
        name="gla",
    )(proj, proj, proj, lr, proj, proj, proj, lr, proj, wgk_pad, bgk, norm_g, masks, masks[:2].astype(BF16))


def _na_kernel(q_ref, k_ref, v_ref, z_ref, bias_ref, o_ref, *, rows):
    win = NA_ROWS * GRID_W
    scale = NA_HEAD_DIM ** -0.5

    def group(g, carry):
        q0s, k0s, scores = [], [], []
        for i in range(NA_GROUP):
            r = g * NA_GROUP + i
            rs = jnp.clip(r - NA_ROWS // 2, 0, rows - NA_ROWS)
            q0 = pl.multiple_of(r * GRID_W, GRID_W)
            k0 = pl.multiple_of(rs * GRID_W, GRID_W)
            s = lax.dot_general(q_ref[pl.ds(q0, GRID_W), :], k_ref[pl.ds(k0, win), :], _NT,
                                preferred_element_type=F32)
            scores.append(s * (scale * LOG2E) + bias_ref[0, r - rs])
            q0s.append(q0)
            k0s.append(k0)
        probs, inv = [], []
        for s in scores:
            p = jnp.exp2(s - jnp.max(s, axis=-1, keepdims=True))
            inv.append(1.0 / jnp.sum(p, axis=-1, keepdims=True))
            probs.append(p.astype(BF16))
        for i in range(NA_GROUP):
            o = jnp.dot(probs[i], v_ref[pl.ds(k0s[i], win), :], preferred_element_type=F32) * inv[i]
            z = z_ref[pl.ds(q0s[i], GRID_W), :].astype(F32)
            o_ref[pl.ds(q0s[i], GRID_W), :] = (o * _silu(z)).astype(o_ref.dtype)
        return carry

    lax.fori_loop(0, rows // NA_GROUP, group, 0)


def _na(proj, bias_tab, batch, seq):
    m = proj.shape[0]
    rows = seq // GRID_W
    assert rows % NA_GROUP == 0 and rows >= NA_ROWS
    hd = NA_HEAD_DIM
    return pl.pallas_call(
        functools.partial(_na_kernel, rows=rows),
        out_shape=jax.ShapeDtypeStruct((m, BRANCH_W), BF16),
        grid=(NA_HEADS, batch),
        in_specs=[pl.BlockSpec((seq, hd), lambda h, b: (b, COL_C_Q // hd + h)),
                  pl.BlockSpec((seq, hd), lambda h, b: (b, COL_C_K // hd + h)),
                  pl.BlockSpec((seq, hd), lambda h, b: (b, COL_C_V // hd + h)),
                  pl.BlockSpec((seq, hd), lambda h, b: (b, COL_C_Z // hd + h)),
                  pl.BlockSpec((1, NA_ROWS, GRID_W, NA_ROWS * GRID_W), lambda h, b: (h, 0, 0, 0))],
        out_specs=pl.BlockSpec((seq, hd), lambda h, b: (b, h)),
        compiler_params=_params(("parallel", "parallel")),
        name="na",
    )(proj, proj, proj, proj, bias_tab)


def _merge_kernel(ya_ref, yb_ref, yc_ref, wb_ref, ga_ref, gb_ref, gc_ref, o_ref):
    y_refs = (ya_ref, yb_ref, yc_ref)
    g_refs = (ga_ref, gb_ref, gc_ref)
    for c in range(o_ref.shape[1] // PROJ_CHUNK):
        cols = slice(c * PROJ_CHUNK, (c + 1) * PROJ_CHUNK)
        acc = None
        for b in range(N_BRANCH):
            gate = jax.nn.sigmoid(g_refs[b][:, cols].astype(F32))
            t = gate * jnp.dot(y_refs[b][...], wb_ref[b, :, cols], preferred_element_type=F32)
            acc = t if acc is None else acc + t
        o_ref[:, cols] = acc.astype(o_ref.dtype)


def _merge(ya, yb, yc, wb, l, gates):
    m = ya.shape[0]
    tm, tn = MERGE_TM, MERGE_TN
    y_spec = pl.BlockSpec((tm, BRANCH_W), lambda i, j: (i, 0))

    def gate_spec(b):
        return pl.BlockSpec((tm, tn), lambda i, j: (i, b * D_MODEL // tn + j))

    return pl.pallas_call(
        _merge_kernel,
        out_shape=jax.ShapeDtypeStruct((m, D_MODEL), BF16),
        grid=(m // tm, D_MODEL // tn),
        in_specs=[y_spec, y_spec, y_spec,
                  pl.BlockSpec((None, N_BRANCH, BRANCH_W, tn), lambda i, j: (l, 0, 0, j)),
                  gate_spec(0), gate_spec(1), gate_spec(2)],
        out_specs=pl.BlockSpec((tm, tn), lambda i, j: (i, j)),
        compiler_params=_params(("parallel", "arbitrary")),
        name="merge",
    )(ya, yb, yc, wb, gates, gates, gates)


def _out_kernel(m_ref, w_ref, x_ref, o_ref):
    _dot_chunks(m_ref, w_ref, o_ref, lambda acc, cols: x_ref[:, cols] + acc)


def _out_prenorm_kernel(m_ref, w_ref, x_ref, g_ref, o_ref, xg_ref, ssq_ref):
    @pl.when(pl.program_id(1) == 0)
    def _():
        ssq_ref[...] = jnp.zeros_like(ssq_ref)

    ssq = None
    for c in range(o_ref.shape[1] // PROJ_CHUNK):
        cols = slice(c * PROJ_CHUNK, (c + 1) * PROJ_CHUNK)
        xn = x_ref[:, cols] + jnp.dot(m_ref[...], w_ref[:, cols], preferred_element_type=F32)
        o_ref[:, cols] = xn
        xg_ref[:, cols] = (xn * g_ref[:, cols]).astype(xg_ref.dtype)
        s = jnp.sum(xn * xn, axis=-1, keepdims=True)
        ssq = s if ssq is None else ssq + s
    ssq_ref[...] += jnp.broadcast_to(ssq, ssq_ref.shape)


def _out(merged, wo, l, x, g_next=None):
    m = x.shape[0]
    tm, tn = OUT_TM, OUT_TN
    in_specs = [pl.BlockSpec((tm, D_MODEL), lambda i, j: (i, 0)),
                pl.BlockSpec((None, D_MODEL, tn), lambda i, j: (l, 0, j)),
                pl.BlockSpec((tm, tn), lambda i, j: (i, j))]
    x_spec = pl.BlockSpec((tm, tn), lambda i, j: (i, j))
    x_shape = jax.ShapeDtypeStruct((m, D_MODEL), F32)
    common = dict(grid=(m // tm, D_MODEL // tn), compiler_params=_params(("parallel", "arbitrary")))
    if g_next is None:
        return pl.pallas_call(_out_kernel, out_shape=x_shape, in_specs=in_specs, out_specs=x_spec,
                              name="outproj", **common)(merged, wo, x)
    return pl.pallas_call(
        _out_prenorm_kernel,
        out_shape=(x_shape, jax.ShapeDtypeStruct((m, D_MODEL), BF16),
                   jax.ShapeDtypeStruct((m, STAT_LANES), F32)),
        in_specs=in_specs + [pl.BlockSpec((1, tn), lambda i, j: (0, j))],
        out_specs=(x_spec, pl.BlockSpec((tm, tn), lambda i, j: (i, j)),
                   pl.BlockSpec((tm, STAT_LANES), lambda i, j: (i, 0))),
        name="outproj_prenorm", **common)(merged, wo, x, g_next.reshape(1, D_MODEL))


def _na_bias_table(rpb):
    cols = jnp.arange(GRID_W)
    cs = jnp.clip(cols - NA_COLS // 2, 0, GRID_W - NA_COLS)
    col_ok = (cols[None, :] >= cs[:, None]) & (cols[None, :] < cs[:, None] + NA_COLS)
    dc = jnp.clip(cols[None, :] - cols[:, None] + NA_COLS - 1, 0, 2 * NA_COLS - 2)
    toe = jnp.zeros(rpb.shape[:3] + (GRID_W, GRID_W), F32)
    for c in range(2 * NA_COLS - 1):
        toe = jnp.where(dc == c, rpb[:, :, :, c, None, None], toe)
    toe = jnp.where(col_ok, toe * LOG2E, NEG_BIAS)
    return jnp.stack(
        [jnp.concatenate([toe[:, :, n - d + NA_ROWS - 1] for n in range(NA_ROWS)], axis=-1)
         for d in range(NA_ROWS)], axis=2)


def _trunk(x, batch, seq, p):
    xg, ssq = _prenorm(x, p["norm_g"][0])
    for l in range(DEPTH):
        pab, lr = _proj_in(xg, ssq, p["w_in_t"], l, AB_COLS, lr_row=AB_COLS)
        pc = _proj_in(xg, ssq, p["w_c_t"], l, C_COLS)
        gates = _proj_gate(xg, ssq, p["w_gate"], p["b_gate"], l)
        ya = _sgu(pab, p["sgu_ln_g"][l], p["sgu_w"][l], p["sgu_b"][l])
        yb = _gla(pab, lr, p["wgk"][l], p["bgk"][l], p["gla_norm_g"][l], batch, seq)
        yc = _na(pc, p["na_bias"][l], batch, seq)
        merged = _merge(ya, yb, yc, p["w_branch"], l, gates)
        if l + 1 < DEPTH:
            x, xg, ssq = _out(merged, p["w_out"], l, x, g_next=p["norm_g"][l + 1])
        else:
            x = _out(merged, p["w_out"], l, x)
    return _rmsnorm(x, p["final_norm_g"], F32)


def kernel(x_prompt, x_sample, norm_g, w_in, sgu_ln_g, sgu_w, sgu_b, gla_w_gk, gla_b_gk, gla_norm_g,
           na_rpb, w_gate, b_gate, w_branch, w_out, final_norm_g):
    w_in_t = jnp.swapaxes(w_in, 1, 2).astype(BF16)
    wgk = jnp.zeros((DEPTH, 2, LR_PAD, GLA_KEY_W), F32)
    wgk = wgk.at[:, 0, :GLA_LOW_RANK].set(gla_w_gk[:, 0])
    wgk = wgk.at[:, 1, GLA_LOW_RANK:2 * GLA_LOW_RANK].set(gla_w_gk[:, 1])
    p = {
        "norm_g": norm_g,
        "final_norm_g": final_norm_g,
        "w_in_t": w_in_t,
        "w_c_t": w_in_t[:, AB_COLS + 2 * GLA_LOW_RANK:],
        "w_gate": w_gate.astype(BF16),
        "b_gate": b_gate.reshape(DEPTH, 1, N_BRANCH * D_MODEL),
        "sgu_ln_g": sgu_ln_g.reshape(DEPTH, 1, BRANCH_W),
        "sgu_w": sgu_w.astype(BF16),
        "sgu_b": jnp.repeat(jnp.swapaxes(sgu_b, 1, 2), SGU_GROUP_W, axis=2),
        "wgk": wgk.astype(BF16),
        "bgk": gla_b_gk.reshape(DEPTH, 2, 1, GLA_KEY_W),
        "gla_norm_g": gla_norm_g.reshape(DEPTH, 1, GLA_HEAD_V),
        "na_bias": _na_bias_table(na_rpb),
        "w_branch": w_branch.astype(BF16),
        "w_out": w_out.astype(BF16),
    }
    bp, tp, _ = x_prompt.shape
    bs, ts, _ = x_sample.shape
    y_prompt = _trunk(x_prompt.reshape(bp * tp, D_MODEL), bp, tp, p).reshape(x_prompt.shape)
    y_sample = _trunk(x_sample.reshape(bs * ts, D_MODEL), bs, ts, p).reshape(x_sample.shape)
    return (y_prompt, y_sample)
```
